```python
import math
import jax
import jax.numpy as jnp
from jax import lax
import numpy as np

D_MODEL = 1024
BATCH = 8
SEQ = 4096
DEPTH = 2
DEC_BATCH = 32
DEC_SEQ = 1
PAST_LEN = 16384
PAGE_SIZE = 128

N_EVEN = (DEPTH + 1) // 2
N_ODD = DEPTH // 2
D_A = D_MODEL // 2
D_B = D_MODEL // 2
CONV_W = 3
CHUNK = 128
G_B = 4
C_B = D_B // G_B
N_HEADS = 8
HEAD_DIM = D_MODEL // N_HEADS // 2
D_ATT = 2 * N_HEADS * HEAD_DIM
ROT_DIM = HEAD_DIM // 4
ROPE_THETA = 500000.0
Q_BLOCK = 128
D_FF = 2816
EPS = 1e-6
SUBLN_EPS = 1e-5
LN_EPS = 1e-5
NEG_INF = -1e30

kernel_name = 'hybrid_conv_chunkmlp_diffattn_macaron_step'


def rms_norm(x, g, eps=EPS):
    x32 = x.astype(jnp.float32)
    y = x32 * lax.rsqrt(jnp.mean(x32 * x32, axis=-1, keepdims=True) + eps)
    return (y * g.astype(jnp.float32)).astype(x.dtype)


def layer_norm(x, g, b, eps=LN_EPS):
    x32 = x.astype(jnp.float32)
    mu = jnp.mean(x32, axis=-1, keepdims=True)
    xc = x32 - mu
    var = jnp.mean(xc * xc, axis=-1, keepdims=True)
    y = xc * lax.rsqrt(var + eps) * g.astype(jnp.float32) + b.astype(jnp.float32)
    return y.astype(x.dtype)


def swiglu(h, wg, wu, wd):
    return (jax.nn.silu(h @ wg) * (h @ wu)) @ wd


def rope_partial(x, pos):
    half = ROT_DIM // 2
    inv = ROPE_THETA ** (-(jnp.arange(half, dtype=jnp.float32) * 2.0) / ROT_DIM)
    ang = pos.astype(jnp.float32)[:, None] * inv[None, :]
    cos = jnp.cos(ang)[None, :, None, :]
    sin = jnp.sin(ang)[None, :, None, :]
    xr = x[..., :ROT_DIM].astype(jnp.float32)
    x1, x2 = xr[..., :half], xr[..., half:]
    rot = jnp.concatenate([x1 * cos - x2 * sin, x2 * cos + x1 * sin], axis=-1).astype(x.dtype)
    return jnp.concatenate([rot, x[..., ROT_DIM:]], axis=-1)


def short_conv_mixer(bg, cg, hx, past, conv_w):
    xc = cg * hx
    t = xc.shape[1]
    xp = jnp.concatenate([past.astype(xc.dtype), xc], axis=1)
    y = conv_w[0] * xp[:, :t] + conv_w[1] * xp[:, 1:t + 1] + conv_w[2] * xp[:, 2:t + 2]
    return bg * y, xp[:, -(CONV_W - 1):]


def chunk_mlp_mixer(u, v, ln_g, ln_b, w_s, b_s):
    vn = layer_norm(v, ln_g, ln_b)
    bn, t, _ = vn.shape
    n_chunks = -(-t // CHUNK)
    tp = n_chunks * CHUNK
    vp = jnp.pad(vn, ((0, 0), (0, tp - t), (0, 0))).reshape(bn, n_chunks, CHUNK, G_B, C_B)
    mask = jnp.tril(jnp.ones((CHUNK, CHUNK), dtype=bool))
    ws = jnp.where(mask[None], w_s, 0.0)
    z = jnp.einsum('gts,bnsgc->bntgc', ws, vp) + b_s.T[None, None, :, :, None]
    z = z.reshape(bn, tp, D_B)[:, :t]
    return u * z, vn


def diff_attend(q, ks, vs, q_pos, k_pos, lam):
    s = jnp.concatenate(
        [jnp.einsum('bqhd,bkhd->bhqk', q, k, preferred_element_type=jnp.float32) for k in ks], axis=-1)
    s = jnp.where(k_pos[None, None, None, :] <= q_pos[None, None, :, None], s, NEG_INF)
    p = jax.nn.softmax(s, axis=-1)
    bn, _, tq, sk = p.shape
    p = p.reshape(bn, N_HEADS, 2, tq, sk)
    a = p[:, :, 0] - lam * p[:, :, 1]
    start = 0
    out = None
    for v in vs:
        seg = v.shape[1]
        o = jnp.einsum('bhqk,bkhe->bqhe', a[..., start:start + seg].astype(v.dtype), v)
        out = o if out is None else out + o
        start += seg
    return out


def diff_attn_mixer(h, pos, w_qkv, lam_qk, subln_g, w_o, lam_init, k_past, v_past):
    bn, t, _ = h.shape
    q, k, v = jnp.split(h @ w_qkv, 3, axis=-1)
    q = rope_partial(q.reshape(bn, t, 2 * N_HEADS, HEAD_DIM), pos) * (HEAD_DIM ** -0.5)
    k = rope_partial(k.reshape(bn, t, 2 * N_HEADS, HEAD_DIM), pos)
    v = v.reshape(bn, t, N_HEADS, 2 * HEAD_DIM)
    lq = lam_qk.astype(jnp.float32)
    lam = jnp.exp(jnp.sum(lq[0] * lq[1])) - jnp.exp(jnp.sum(lq[2] * lq[3])) + lam_init
    if k_past is None:
        nb = t // Q_BLOCK
        qb = q.reshape(bn, nb, Q_BLOCK, 2 * N_HEADS, HEAD_DIM).swapaxes(0, 1)
        pb = pos.reshape(nb, Q_BLOCK)
        o = lax.map(lambda a: diff_attend(a[0], [k], [v], a[1], pos, lam), (qb, pb))
        o = o.swapaxes(0, 1).reshape(bn, t, N_HEADS, 2 * HEAD_DIM)
    else:
        past_pos = jnp.arange(k_past.shape[1])
        o = diff_attend(q, [k_past, k], [v_past, v], pos, jnp.concatenate([past_pos, pos]), lam)
    o = rms_norm(o, subln_g, SUBLN_EPS) * (1.0 - lam_init)
    return o.reshape(bn, t, D_ATT) @ w_o, k, v


def trunk(x, pos, conv_past, cache_k, cache_v, page_table, norm_g, ffn_w_gate, ffn_w_up, ffn_w_down,
          w_in_mix, conv_w, sg_ln_g, sg_ln_b, sg_w, sg_b, w_out_mix, w_qkv, lambda_qk, subln_g, w_o,
          final_norm_g):
    bn = x.shape[0]
    conv_states, chunk_vs, k_rows, v_rows = [], [], [], []
    for l in range(DEPTH):
        x = x + 0.5 * swiglu(rms_norm(x, norm_g[l, 0]), ffn_w_gate[l, 0], ffn_w_up[l, 0], ffn_w_down[l, 0])
        h = rms_norm(x, norm_g[l, 1])
        if l % 2 == 0:
            i = l // 2
            bg, cg, hx, u, v = jnp.split(h @ w_in_mix[i], [D_A, 2 * D_A, 3 * D_A, 3 * D_A + D_B], axis=-1)
            ya, st = short_conv_mixer(bg, cg, hx, conv_past[i], conv_w[i])
            yb, vn = chunk_mlp_mixer(u, v, sg_ln_g[i], sg_ln_b[i], sg_w[i], sg_b[i])
            mix = jnp.concatenate([ya, yb], axis=-1) @ w_out_mix[i]
            conv_states.append(st)
            chunk_vs.append(vn)
        else:
            j = l // 2
            lam_init = 0.8 - 0.6 * math.exp(-0.3 * l)
            if page_table is None:
                kp, vp = None, None
            else:
                kp = cache_k[j, page_table].reshape(bn, -1, 2 * N_HEADS, HEAD_DIM)
                vp = cache_v[j, page_table].reshape(bn, -1, N_HEADS, 2 * HEAD_DIM)
            mix, kn, vnew = diff_attn_mixer(h, pos, w_qkv[j], lambda_qk[j], subln_g[j], w_o[j], lam_init, kp, vp)
            k_rows.append(kn)
            v_rows.append(vnew)
        x = x + mix
        x = x + 0.5 * swiglu(rms_norm(x, norm_g[l, 2]), ffn_w_gate[l, 1], ffn_w_up[l, 1], ffn_w_down[l, 1])
    return rms_norm(x, final_norm_g), conv_states, chunk_vs, k_rows, v_rows


def setup_inputs(seed: int = 0) -> dict:
    key = jax.random.key(seed)
    ks = jax.random.split(key, 24)
    f32 = jnp.float32
    n_pages = PAST_LEN // PAGE_SIZE
    n_used = DEC_BATCH * n_pages
    n_phys = n_used + max(1, n_used // 4)

    def nrm(k, shape, scale):
        return jax.random.normal(k, shape, f32) * scale

    page_table = jax.random.permutation(ks[5], n_phys)[:n_used].reshape(DEC_BATCH, n_pages).astype(jnp.int32)
    return {
        'x_prompt': nrm(ks[0], (BATCH, SEQ, D_MODEL), 1.0),
        'x_sample': nrm(ks[1], (DEC_BATCH, DEC_SEQ, D_MODEL), 1.0),
        'cache_conv': nrm(ks[2], (N_EVEN, DEC_BATCH, CONV_W - 1, D_A), 1.0),
        'cache_k': nrm(ks[3], (N_ODD, n_phys, PAGE_SIZE, 2 * N_HEADS, HEAD_DIM), 1.0),
        'cache_v': nrm(ks[4], (N_ODD, n_phys, PAGE_SIZE, N_HEADS, 2 * HEAD_DIM), 1.0),
        'page_table': page_table,
        'norm_g': 1.0 + nrm(ks[6], (DEPTH, 3, D_MODEL), 0.02),
        'ffn_w_gate': nrm(ks[7], (DEPTH, 2, D_MODEL, D_FF), D_MODEL ** -0.5),
        'ffn_w_up': nrm(ks[8], (DEPTH, 2, D_MODEL, D_FF), D_MODEL ** -0.5),
        'ffn_w_down': nrm(ks[9], (DEPTH, 2, D_FF, D_MODEL), D_FF ** -0.5),
        'w_in_mix': nrm(ks[10], (N_EVEN, D_MODEL, 3 * D_A + 2 * D_B), D_MODEL ** -0.5),
        'conv_w': nrm(ks[11], (N_EVEN, CONV_W, D_A), CONV_W ** -0.5),
        'sg_ln_g': 1.0 + nrm(ks[12], (N_EVEN, D_B), 0.02),
        'sg_ln_b': nrm(ks[13], (N_EVEN, D_B), 0.02),
        'sg_w': nrm(ks[14], (N_EVEN, G_B, CHUNK, CHUNK), CHUNK ** -0.5),
        'sg_b': 1.0 + nrm(ks[15], (N_EVEN, G_B, CHUNK), 0.02),
        'w_out_mix': nrm(ks[16], (N_EVEN, D_A + D_B, D_MODEL), (D_A + D_B) ** -0.5),
        'w_qkv': nrm(ks[17], (N_ODD, D_MODEL, 3 * D_ATT), D_MODEL ** -0.5),
        'lambda_qk': nrm(ks[18], (N_ODD, 4, HEAD_DIM), 0.1),
        'subln_g': 1.0 + nrm(ks[19], (N_ODD, 2 * HEAD_DIM), 0.02),
        'w_o': nrm(ks[20], (N_ODD, D_ATT, D_MODEL), D_ATT ** -0.5),
        'final_norm_g': 1.0 + nrm(ks[21], (D_MODEL,), 0.02),
    }


def reference(x_prompt, x_sample, cache_conv, cache_k, cache_v, page_table, norm_g, ffn_w_gate, ffn_w_up,
              ffn_w_down, w_in_mix, conv_w, sg_ln_g, sg_ln_b, sg_w, sg_b, w_out_mix, w_qkv, lambda_qk,
              subln_g, w_o, final_norm_g):
    n_prompt, seq = x_prompt.shape[0], x_prompt.shape[1]
    n_dec, dec_seq = x_sample.shape[0], x_sample.shape[1]
    pos_prompt = jnp.arange(seq)
    pos_sample = PAST_LEN + jnp.arange(dec_seq)
    conv_zero = jnp.zeros((N_EVEN, n_prompt, CONV_W - 1, D_A), x_prompt.dtype)
    y_prompt, conv_p, _, k_p, v_p = trunk(
        x_prompt, pos_prompt, conv_zero, None, None, None, norm_g, ffn_w_gate, ffn_w_up, ffn_w_down,
        w_in_mix, conv_w, sg_ln_g, sg_ln_b, sg_w, sg_b, w_out_mix, w_qkv, lambda_qk, subln_g, w_o, final_norm_g)
    y_sample, conv_s, chunk_s, k_s, v_s = trunk(
        x_sample, pos_sample, cache_conv, cache_k, cache_v, page_table, norm_g, ffn_w_gate, ffn_w_up,
        ffn_w_down, w_in_mix, conv_w, sg_ln_g, sg_ln_b, sg_w, sg_b, w_out_mix, w_qkv, lambda_qk, subln_g,
        w_o, final_norm_g)
    conv_state_prompt = jnp.stack(conv_p)
    conv_state_sample = jnp.stack(conv_s)
    chunk_v_sample = jnp.stack(chunk_s)
    k_prompt = jnp.stack(k_p)
    v_prompt = jnp.stack(v_p)
    k_sample = jnp.stack(k_s)
    v_sample = jnp.stack(v_s)
    return (y_prompt, y_sample, conv_state_prompt, conv_state_sample, chunk_v_sample,
            k_prompt, v_prompt, k_sample, v_sample)
```

```python
import functools
import math

import jax
import jax.numpy as jnp
from jax import lax
from jax.experimental import pallas as pl
from jax.experimental.pallas import tpu as pltpu

F32 = jnp.float32
BF16 = jnp.bfloat16

D_MODEL = 1024
D_A = D_MODEL // 2
D_B = D_MODEL // 2
CONV_W = 3
CHUNK = 128
G_B = 4
N_HEADS = 8
HEAD_DIM = 64
V_DIM = 2 * HEAD_DIM
ROT_DIM = HEAD_DIM // 4
ROPE_THETA = 500000.0
D_FF = 2816
EPS = 1e-6
SUBLN_EPS = 1e-5
LN_EPS = 1e-5
NEG_INF = -1e30

LANES = 128
VMEM_LIMIT = 56 * 1024 * 1024
PAGES_PER_STEP = 8


def _rms(x, g, eps):
    ms = jnp.mean(x * x, axis=-1, keepdims=True)
    return x * lax.rsqrt(ms + eps) * g


def _const_spec(shape):
    return pl.BlockSpec(shape, lambda *_: (0,) * len(shape), pipeline_mode=pl.Buffered(1))


def _params(n_axes):
    return pltpu.CompilerParams(dimension_semantics=("arbitrary",) * n_axes,
                                vmem_limit_bytes=VMEM_LIMIT)


def _ffn_body(x_ref, g_ref, wg_ref, wu_ref, wd_ref, *rest, ff_chunk, has_final):
    o_ref = rest[-1]
    x = x_ref[...]
    h = _rms(x, g_ref[...], EPS).astype(BF16)
    acc = None
    for c in range(D_FF // ff_chunk):
        sl = slice(c * ff_chunk, (c + 1) * ff_chunk)
        gate = jnp.dot(h, wg_ref[:, sl], preferred_element_type=F32)
        up = jnp.dot(h, wu_ref[:, sl], preferred_element_type=F32)
        a = (gate * jax.nn.sigmoid(gate) * up).astype(BF16)
        d = jnp.dot(a, wd_ref[sl, :], preferred_element_type=F32)
        acc = d if acc is None else acc + d
    y = x + 0.5 * acc
    if has_final:
        y = _rms(y, rest[0][...], EPS)
    o_ref[...] = y


def _ffn(x, g, wg, wu, wd, final_g, *, tm, ff_chunk):
    n = x.shape[0]
    has_final = final_g is not None
    row = pl.BlockSpec((tm, D_MODEL), lambda i: (i, 0))
    in_specs = [row, _const_spec((1, D_MODEL)), _const_spec((D_MODEL, D_FF)),
                _const_spec((D_MODEL, D_FF)), _const_spec((D_FF, D_MODEL))]
    args = [x, g.reshape(1, D_MODEL), wg, wu, wd]
    if has_final:
        in_specs.append(_const_spec((1, D_MODEL)))
        args.append(final_g.reshape(1, D_MODEL))
    return pl.pallas_call(
        functools.partial(_ffn_body, ff_chunk=ff_chunk, has_final=has_final),
        out_shape=jax.ShapeDtypeStruct((n, D_MODEL), F32),
        grid=(n // tm,), in_specs=in_specs, out_specs=row,
        compiler_params=_params(1), name="ffn")(*args)


def _layer_norm(v, g, b):
    mu = jnp.mean(v, axis=-1, keepdims=True)
    vc = v - mu
    var = jnp.mean(vc * vc, axis=-1, keepdims=True)
    return vc * lax.rsqrt(var + LN_EPS) * g + b


def _mix_prompt_body(x_ref, g_ref, win_ref, cw_ref, lng_ref, lnb_ref, sgw_ref, sgb_ref, wout_ref,
                     o_ref, st_ref, xc_scr, *, tm, tiles_per_seq):
    i = pl.program_id(0)
    x = x_ref[...]
    h = _rms(x, g_ref[...], EPS).astype(BF16)
    proj = jnp.dot(h, win_ref[...], preferred_element_type=F32)
    bg = proj[:, 0:D_A]
    cg = proj[:, D_A:2 * D_A]
    hx = proj[:, 2 * D_A:3 * D_A]
    u = proj[:, 3 * D_A:3 * D_A + D_B]
    v = proj[:, 3 * D_A + D_B:]

    xc = cg * hx

    @pl.when(i % tiles_per_seq == 0)
    def _():
        xc_scr[0:8, :] = jnp.zeros((8, D_A), F32)

    xc_scr[8:8 + tm, :] = xc
    xm1 = xc_scr[7:7 + tm, :]
    xm2 = xc_scr[6:6 + tm, :]
    cw = cw_ref[...]
    ya = bg * (cw[0:1, :] * xm2 + cw[1:2, :] * xm1 + cw[2:3, :] * xc)
    xc_scr[0:8, :] = xc[tm - 8:tm, :]
    st_ref[0] = xc[tm - (CONV_W - 1):tm, :]

    vn = _layer_norm(v, lng_ref[...], lnb_ref[...]).astype(BF16)
    n_chunks = tm // CHUNK
    r_i = lax.broadcasted_iota(jnp.int32, (CHUNK, CHUNK), 0)
    c_i = lax.broadcasted_iota(jnp.int32, (CHUNK, CHUNK), 1)
    zs = []
    for gi in range(G_B):
        w_tril = jnp.where(c_i <= r_i, sgw_ref[gi], 0.0).astype(BF16)
        vg = vn[:, gi * LANES:(gi + 1) * LANES]
        wide = jnp.concatenate([vg[c * CHUNK:(c + 1) * CHUNK, :] for c in range(n_chunks)], axis=1)
        zw = jnp.dot(w_tril, wide, preferred_element_type=F32)
        bias = sgb_ref[gi]
        zs.append(jnp.concatenate(
            [zw[:, c * CHUNK:(c + 1) * CHUNK] + bias for c in range(n_chunks)], axis=0))
    yb = u * jnp.concatenate(zs, axis=1)

    mixin = jnp.concatenate([ya, yb], axis=1).astype(BF16)
    o_ref[...] = x + jnp.dot(mixin, wout_ref[...], preferred_element_type=F32)


def _mix_prompt(x, g, win, cw, lng, lnb, sgw, sgb_tbl, wout, *, n_seq, seq, tm):
    n = x.shape[0]
    tiles_per_seq = seq // tm
    d_in = 3 * D_A + 2 * D_B
    row = pl.BlockSpec((tm, D_MODEL), lambda i: (i, 0))
    in_specs = [row, _const_spec((1, D_MODEL)), _const_spec((D_MODEL, d_in)),
                _const_spec((CONV_W, D_A)), _const_spec((1, D_B)), _const_spec((1, D_B)),
                _const_spec((G_B, CHUNK, CHUNK)), _const_spec((G_B, CHUNK, LANES)),
                _const_spec((D_A + D_B, D_MODEL))]
    out_specs = [row, pl.BlockSpec((1, CONV_W - 1, D_A), lambda i: (i // tiles_per_seq, 0, 0))]
    return pl.pallas_call(
        functools.partial(_mix_prompt_body, tm=tm, tiles_per_seq=tiles_per_seq),
        out_shape=[jax.ShapeDtypeStruct((n, D_MODEL), F32),
                   jax.ShapeDtypeStruct((n_seq, CONV_W - 1, D_A), F32)],
        grid=(n // tm,), in_specs=in_specs, out_specs=out_specs,
        scratch_shapes=[pltpu.VMEM((tm + 8, D_A), F32)],
        compiler_params=_params(1), name="mix_prompt")(
            x, g.reshape(1, D_MODEL), win, cw, lng.reshape(1, D_B), lnb.reshape(1, D_B),
            sgw, sgb_tbl, wout)


def _mix_decode_body(x_ref, g_ref, win_ref, cw_ref, past_ref, lng_ref, lnb_ref, w00_ref, b0_ref,
                     wout_ref, o_ref, st_ref, vn_ref):
    x = x_ref[...]
    h = _rms(x, g_ref[...], EPS).astype(BF16)
    proj = jnp.dot(h, win_ref[...], preferred_element_type=F32)
    bg = proj[:, 0:D_A]
    cg = proj[:, D_A:2 * D_A]
    hx = proj[:, 2 * D_A:3 * D_A]
    u = proj[:, 3 * D_A:3 * D_A + D_B]
    v = proj[:, 3 * D_A + D_B:]
    xc = cg * hx
    p0 = past_ref[:, 0:D_A]
    p1 = past_ref[:, D_A:2 * D_A]
    cw = cw_ref[...]
    ya = bg * (cw[0:1, :] * p0 + cw[1:2, :] * p1 + cw[2:3, :] * xc)
    st_ref[:, 0:D_A] = p1
    st_ref[:, D_A:2 * D_A] = xc
    vn = _layer_norm(v, lng_ref[...], lnb_ref[...])
    vn_ref[...] = vn
    yb = u * (w00_ref[...] * vn + b0_ref[...])
    mixin = jnp.concatenate([ya, yb], axis=1).astype(BF16)
    o_ref[...] = x + jnp.dot(mixin, wout_ref[...], preferred_element_type=F32)


def _mix_decode(x, g, win, cw, past, lng, lnb, w00, b0, wout):
    n = x.shape[0]
    d_in = 3 * D_A + 2 * D_B
    full = lambda shape: pl.BlockSpec(shape, lambda i: (0,) * len(shape))
    return pl.pallas_call(
        _mix_decode_body,
        out_shape=[jax.ShapeDtypeStruct((n, D_MODEL), F32),
                   jax.ShapeDtypeStruct((n, (CONV_W - 1) * D_A), F32),
                   jax.ShapeDtypeStruct((n, D_B), F32)],
        grid=(1,),
        in_specs=[full((n, D_MODEL)), full((1, D_MODEL)), full((D_MODEL, d_in)),
                  full((CONV_W, D_A)), full((n, (CONV_W - 1) * D_A)), full((1, D_B)),
                  full((1, D_B)), full((1, D_B)), full((1, D_B)), full((D_A + D_B, D_MODEL))],
        out_specs=[full((n, D_MODEL)), full((n, (CONV_W - 1) * D_A)), full((n, D_B))],
        compiler_params=_params(1), name="mix_decode")(
            x, g.reshape(1, D_MODEL), win, cw, past, lng.reshape(1, D_B), lnb.reshape(1, D_B),
            w00, b0, wout)


def _rope_tables(pos):
    half = ROT_DIM // 2
    inv = ROPE_THETA ** (-(jnp.arange(half, dtype=F32) * 2.0) / ROT_DIM)
    ang = pos.astype(F32)[:, None] * inv[None, :]
    cos, sin = jnp.cos(ang), jnp.sin(ang)
    t = pos.shape[0]
    pad = jnp.zeros((t, HEAD_DIM - ROT_DIM), F32)
    cos64 = jnp.concatenate([cos, cos, pad + 1.0], axis=1)
    sa64 = jnp.concatenate([jnp.zeros_like(sin), sin, pad], axis=1)
    sb64 = jnp.concatenate([-sin, jnp.zeros_like(sin), pad], axis=1)
    two = lambda a: jnp.concatenate([a, a], axis=1)
    return two(cos64), two(sa64), two(sb64)


def _rope(t, cos, sa, sb):
    blocks = []
    for b in range(t.shape[1] // LANES):
        blk = t[:, b * LANES:(b + 1) * LANES]
        blocks.append(blk * cos + pltpu.roll(blk, ROT_DIM // 2, 1) * sa
                      + pltpu.roll(blk, LANES - ROT_DIM // 2, 1) * sb)
    return jnp.concatenate(blocks, axis=1)


def _qkv_body(x_ref, g_ref, w_ref, cos_ref, sa_ref, sb_ref, q_ref, k_ref, v_ref, kb_ref, vb_ref):
    d_att = 2 * N_HEADS * HEAD_DIM
    h = _rms(x_ref[...], g_ref[...], EPS).astype(BF16)
    qkv = jnp.dot(h, w_ref[...], preferred_element_type=F32)
    cos, sa, sb = cos_ref[...], sa_ref[...], sb_ref[...]
    q = _rope(qkv[:, 0:d_att], cos, sa, sb) * (HEAD_DIM ** -0.5)
    k = _rope(qkv[:, d_att:2 * d_att], cos, sa, sb)
    v = qkv[:, 2 * d_att:]
    q_ref[...] = q.astype(q_ref.dtype)
    k_ref[...] = k
    v_ref[...] = v
    kb_ref[...] = k.astype(BF16)
    vb_ref[...] = v.astype(BF16)


def _qkv(x, g, w, tables, *, tm, seq_tiles, q_dtype):
    n = x.shape[0]
    d_att = 2 * N_HEADS * HEAD_DIM
    row = pl.BlockSpec((tm, d_att), lambda i: (i, 0))
    tbl = pl.BlockSpec((tm, LANES), lambda i: (i % seq_tiles, 0))
    return pl.pallas_call(
        _qkv_body,
        out_shape=[jax.ShapeDtypeStruct((n, d_att), q_dtype), jax.ShapeDtypeStruct((n, d_att), F32),
                   jax.ShapeDtypeStruct((n, d_att), F32), jax.ShapeDtypeStruct((n, d_att), BF16),
                   jax.ShapeDtypeStruct((n, d_att), BF16)],
        grid=(n // tm,),
        in_specs=[pl.BlockSpec((tm, D_MODEL), lambda i: (i, 0)), _const_spec((1, D_MODEL)),
                  _const_spec((D_MODEL, 3 * d_att)), tbl, tbl, tbl],
        out_specs=[row, row, row, row, row],
        compiler_params=_params(1), name="qkv_rope")(x, g.reshape(1, D_MODEL), w, *tables)


def _lambda(lq, lam_init):
    a = jnp.sum(lq[0:1, :] * lq[1:2, :], axis=-1, keepdims=True)
    b = jnp.sum(lq[2:3, :] * lq[3:4, :], axis=-1, keepdims=True)
    return jnp.exp(a) - jnp.exp(b) + lam_init


def _flash_body(lq_ref, sg_ref, q_ref, k_ref, v_ref, o_ref, q2_scr, m_scr, l_scr, acc_scr,
                *, tq, lam_init):
    qi = pl.program_id(2)
    q = q_ref[0]
    lane = lax.broadcasted_iota(jnp.int32, (tq, V_DIM), 1)
    zero = jnp.zeros_like(q)
    q2_scr[0:tq, :] = jnp.where(lane < HEAD_DIM, q, zero)
    q2_scr[tq:2 * tq, :] = jnp.where(lane >= HEAD_DIM, q, zero)
    m_scr[...] = jnp.full((2 * tq, 1), NEG_INF, F32)
    l_scr[...] = jnp.zeros((2 * tq, 1), F32)
    acc_scr[...] = jnp.zeros((2 * tq, V_DIM), F32)

    def step(j, masked):
        start = pl.multiple_of(j * tq, tq)
        kc = k_ref[0, pl.ds(start, tq), :]
        vc = v_ref[0, pl.ds(start, tq), :]
        s = lax.dot_general(q2_scr[...], kc, (((1,), (1,)), ((), ())),
                            preferred_element_type=F32)
        if masked:
            r_i = lax.broadcasted_iota(jnp.int32, (2 * tq, tq), 0)
            c_i = lax.broadcasted_iota(jnp.int32, (2 * tq, tq), 1)
            r_i = jnp.where(r_i >= tq, r_i - tq, r_i)
            s = jnp.where(c_i <= r_i, s, NEG_INF)
        m_prev = m_scr[...]
        m_new = jnp.maximum(m_prev, jnp.max(s, axis=-1, keepdims=True))
        alpha = jnp.exp(m_prev - m_new)
        p = jnp.exp(s - m_new)
        l_scr[...] = alpha * l_scr[...] + jnp.sum(p, axis=-1, keepdims=True)
        acc_scr[...] = alpha * acc_scr[...] + jnp.dot(p.astype(BF16), vc,
                                                      preferred_element_type=F32)
        m_scr[...] = m_new

    def loop_body(j, carry):
        step(j, False)
        return carry

    lax.fori_loop(0, qi, loop_body, 0)
    step(qi, True)

    o2 = acc_scr[...] / l_scr[...]
    lam = _lambda(lq_ref[...], lam_init)
    o = o2[0:tq, :] - lam * o2[tq:2 * tq, :]
    o = _rms(o, sg_ref[...], SUBLN_EPS) * (1.0 - lam_init)
    o_ref[0] = o.astype(o_ref.dtype)


def _flash(q, k, v, lq, sg, *, n_seq, seq, tq, lam_init):
    d_att = 2 * N_HEADS * HEAD_DIM
    q3, k3, v3 = (a.reshape(n_seq, seq, d_att) for a in (q, k, v))
    kv_spec = pl.BlockSpec((1, seq, V_DIM), lambda b, h, i: (b, 0, h))
    q_spec = pl.BlockSpec((1, tq, V_DIM), lambda b, h, i: (b, i, h))
    out = pl.pallas_call(
        functools.partial(_flash_body, tq=tq, lam_init=lam_init),
        out_shape=jax.ShapeDtypeStruct((n_seq, seq, d_att), BF16),
        grid=(n_seq, N_HEADS, seq // tq),
        in_specs=[pl.BlockSpec((4, HEAD_DIM), lambda b, h, i: (0, 0)),
                  pl.BlockSpec((1, V_DIM), lambda b, h, i: (0, 0)), q_spec, kv_spec, kv_spec],
        out_specs=q_spec,
        scratch_shapes=[pltpu.VMEM((2 * tq, V_DIM), BF16), pltpu.VMEM((2 * tq, 1), F32),
                        pltpu.VMEM((2 * tq, 1), F32), pltpu.VMEM((2 * tq, V_DIM), F32)],
        compiler_params=_params(3), name="flash_diff")(lq, sg.reshape(1, V_DIM), q3, k3, v3)
    return out.reshape(n_seq * seq, d_att)


def _decode_attn_body(pt_ref, lq_ref, q_ref, kn_ref, vn_ref, *refs, page, lam_init):
    del pt_ref
    n_sub = 2 * N_HEADS
    d_att = n_sub * HEAD_DIM
    k_refs = refs[:PAGES_PER_STEP]
    v_refs = refs[PAGES_PER_STEP:2 * PAGES_PER_STEP]
    o_ref, qx_scr, m_scr, l_scr, acc_scr = refs[2 * PAGES_PER_STEP:]
    j = pl.program_id(1)
    sub = lax.broadcasted_iota(jnp.int32, (n_sub, d_att), 0)
    lane = lax.broadcasted_iota(jnp.int32, (n_sub, d_att), 1)

    @pl.when(j == 0)
    def _():
        qx_scr[...] = jnp.where(lax.shift_right_logical(lane, 6) == sub, q_ref[0], 0.0)
        m_scr[...] = jnp.full((n_sub, 1), NEG_INF, F32)
        l_scr[...] = jnp.zeros((n_sub, 1), F32)
        acc_scr[...] = jnp.zeros((n_sub, d_att), F32)

    qx = qx_scr[...].astype(BF16)
    s = jnp.concatenate(
        [jnp.dot(qx, k_refs[r][...].astype(BF16), preferred_element_type=F32)
         for r in range(PAGES_PER_STEP)], axis=1)
    m_prev = m_scr[...]
    m_new = jnp.maximum(m_prev, jnp.max(s, axis=-1, keepdims=True))
    alpha = jnp.exp(m_prev - m_new)
    p = jnp.exp(s - m_new)
    l_scr[...] = alpha * l_scr[...] + jnp.sum(p, axis=-1, keepdims=True)
    pb = p.astype(BF16)
    pv = None
    for r in range(PAGES_PER_STEP):
        v_page = jnp.concatenate(
            [v_refs[r][pl.ds(hd, page, stride=N_HEADS), :] for hd in range(N_HEADS)], axis=1)
        d = jnp.dot(pb[:, r * page:(r + 1) * page], v_page.astype(BF16),
                    preferred_element_type=F32)
        pv = d if pv is None else pv + d
    acc_scr[...] = alpha * acc_scr[...] + pv
    m_scr[...] = m_new

    @pl.when(j == pl.num_programs(1) - 1)
    def _():
        s_new = jnp.sum(qx_scr[...] * kn_ref[0], axis=-1, keepdims=True)
        m_prev = m_scr[...]
        m_fin = jnp.maximum(m_prev, s_new)
        alpha = jnp.exp(m_prev - m_fin)
        p_new = jnp.exp(s_new - m_fin)
        l_fin = alpha * l_scr[...] + p_new
        a = (alpha * acc_scr[...] + p_new * vn_ref[0]) / l_fin
        lam = _lambda(lq_ref[...], lam_init)
        w = jnp.where((sub & 1) == 0, 1.0, -lam)
        own = lax.shift_right_logical(lane, 7) == lax.shift_right_logical(sub, 1)
        o_ref[0] = jnp.sum(jnp.where(own, a * w, 0.0), axis=0, keepdims=True)


def _decode_attn(q, k_new, v_new, cache_k, cache_v, page_table, lq, *, lam_init):
    n, d_att = q.shape
    n_phys, page = cache_k.shape[0], cache_k.shape[1]
    n_pages = page_table.shape[1]
    n_sub = 2 * N_HEADS
    ck = jnp.transpose(cache_k, (0, 2, 3, 1)).reshape(n_phys, d_att, page)
    cv = cache_v.reshape(n_phys, page * N_HEADS, V_DIM)
    row = pl.BlockSpec((1, 1, d_att), lambda b, j, pt: (b, 0, 0))

    def page_spec(shape, r):
        return pl.BlockSpec((None,) + shape,
                            lambda b, j, pt: (pt[b, j * PAGES_PER_STEP + r], 0, 0))

    grid_spec = pltpu.PrefetchScalarGridSpec(
        num_scalar_prefetch=1, grid=(n, n_pages // PAGES_PER_STEP),
        in_specs=[pl.BlockSpec((4, HEAD_DIM), lambda b, j, pt: (0, 0)), row, row, row]
        + [page_spec((d_att, page), r) for r in range(PAGES_PER_STEP)]
        + [page_spec((page * N_HEADS, V_DIM), r) for r in range(PAGES_PER_STEP)],
        out_specs=row,
        scratch_shapes=[pltpu.VMEM((n_sub, d_att), F32), pltpu.VMEM((n_sub, 1), F32),
                        pltpu.VMEM((n_sub, 1), F32), pltpu.VMEM((n_sub, d_att), F32)])
    out = pl.pallas_call(
        functools.partial(_decode_attn_body, page=page, lam_init=lam_init),
        out_shape=jax.ShapeDtypeStruct((n, 1, d_att), F32), grid_spec=grid_spec,
        compiler_params=_params(2), name="decode_attn")(
            page_table, lq, q.reshape(n, 1, d_att), k_new.reshape(n, 1, d_att),
            v_new.reshape(n, 1, d_att), *([ck] * PAGES_PER_STEP), *([cv] * PAGES_PER_STEP))
    return out.reshape(n, d_att)


def _proj_res_body(x_ref, o_ref_in, w_ref, *rest, lam_init, subln):
    out_ref = rest[-1]
    o = o_ref_in[...]
    if subln:
        sg = rest[0][...]
        blocks = []
        for hd in range(N_HEADS):
            blk = o[:, hd * V_DIM:(hd + 1) * V_DIM]
            blocks.append(_rms(blk, sg, SUBLN_EPS) * (1.0 - lam_init))
        o = jnp.concatenate(blocks, axis=1)
    out_ref[...] = x_ref[...] + jnp.dot(o.astype(BF16), w_ref[...], preferred_element_type=F32)


def _proj_res(x, o, w, sg, *, tm, lam_init):
    n = x.shape[0]
    d_att = o.shape[1]
    subln = sg is not None
    row = pl.BlockSpec((tm, D_MODEL), lambda i: (i, 0))
    in_specs = [row, pl.BlockSpec((tm, d_att), lambda i: (i, 0)), _const_spec((d_att, D_MODEL))]
    args = [x, o, w]
    if subln:
        in_specs.append(_const_spec((1, V_DIM)))
        args.append(sg.reshape(1, V_DIM))
    return pl.pallas_call(
        functools.partial(_proj_res_body, lam_init=lam_init, subln=subln),
        out_shape=jax.ShapeDtypeStruct((n, D_MODEL), F32),
        grid=(n // tm,), in_specs=in_specs, out_specs=row,
        compiler_params=_params(1), name="attn_out")(*args)


def kernel(x_prompt, x_sample, cache_conv, cache_k, cache_v, page_table, norm_g, ffn_w_gate,
           ffn_w_up, ffn_w_down, w_in_mix, conv_w, sg_ln_g, sg_ln_b, sg_w, sg_b, w_out_mix, w_qkv,
           lambda_qk, subln_g, w_o, final_norm_g):
    n_seq, seq, _ = x_prompt.shape
    n_dec, dec_seq, _ = x_sample.shape
    assert dec_seq == 1 and seq % 512 == 0
    depth = norm_g.shape[0]
    past_len = page_table.shape[1] * cache_k.shape[2]
    assert past_len % CHUNK == 0

    wg, wu, wd = ffn_w_gate.astype(BF16), ffn_w_up.astype(BF16), ffn_w_down.astype(BF16)
    win, wout = w_in_mix.astype(BF16), w_out_mix.astype(BF16)
    wqkv, wo = w_qkv.astype(BF16), w_o.astype(BF16)

    xp = x_prompt.reshape(n_seq * seq, D_MODEL)
    xs = x_sample.reshape(n_dec, D_MODEL)
    tm = 512
    ffn_p = functools.partial(_ffn, tm=tm, ff_chunk=256)
    ffn_s = functools.partial(_ffn, tm=n_dec, ff_chunk=D_FF)
    tables_p = _rope_tables(jnp.arange(seq))
    tables_s = _rope_tables(jnp.broadcast_to(past_len + jnp.arange(dec_seq), (n_dec,)))

    conv_p, conv_s, chunk_s, k_p, v_p, k_s, v_s = [], [], [], [], [], [], []
    for l in range(depth):
        last = l == depth - 1
        xp = ffn_p(xp, norm_g[l, 0], wg[l, 0], wu[l, 0], wd[l, 0], None)
        xs = ffn_s(xs, norm_g[l, 0], wg[l, 0], wu[l, 0], wd[l, 0], None)
        if l % 2 == 0:
            i = l // 2
            sgb_tbl = jnp.broadcast_to(sg_b[i][:, :, None], (G_B, CHUNK, LANES))
            xp, st = _mix_prompt(xp, norm_g[l, 1], win[i], conv_w[i], sg_ln_g[i], sg_ln_b[i],
                                 sg_w[i], sgb_tbl, wout[i], n_seq=n_seq, seq=seq, tm=tm)
            conv_p.append(st)
            w00 = jnp.repeat(sg_w[i][:, 0, 0], D_B // G_B).reshape(1, D_B)
            b0 = jnp.repeat(sg_b[i][:, 0], D_B // G_B).reshape(1, D_B)
            xs, st, vn = _mix_decode(xs, norm_g[l, 1], win[i], conv_w[i],
                                     cache_conv[i].reshape(n_dec, (CONV_W - 1) * D_A),
                                     sg_ln_g[i], sg_ln_b[i], w00, b0, wout[i])
            conv_s.append(st.reshape(n_dec, CONV_W - 1, D_A))
            chunk_s.append(vn.reshape(n_dec, dec_seq, D_B))
        else:
            j = l // 2
            lam_init = 0.8 - 0.6 * math.exp(-0.3 * l)
            q, k, v, kb, vb = _qkv(xp, norm_g[l, 1], wqkv[j], tables_p, tm=tm,
                                   seq_tiles=seq // tm, q_dtype=BF16)
            o = _flash(q, kb, vb, lambda_qk[j], subln_g[j], n_seq=n_seq, seq=seq, tq=512,
                       lam_init=lam_init)
            xp = _proj_res(xp, o, wo[j], None, tm=tm, lam_init=lam_init)
            k_p.append(k.reshape(n_seq, seq, 2 * N_HEADS, HEAD_DIM))
            v_p.append(v.reshape(n_seq, seq, N_HEADS, V_DIM))

            q, k, v, _, _ = _qkv(xs, norm_g[l, 1], wqkv[j], tables_s, tm=n_dec, seq_tiles=1,
                                 q_dtype=F32)
            o = _decode_attn(q, k, v, cache_k[j], cache_v[j], page_table, lambda_qk[j],
                             lam_init=lam_init)
            xs = _proj_res(xs, o, wo[j], subln_g[j], tm=n_dec, lam_init=lam_init)
            k_s.append(k.reshape(n_dec, dec_seq, 2 * N_HEADS, HEAD_DIM))
            v_s.append(v.reshape(n_dec, dec_seq, N_HEADS, V_DIM))
        fg = final_norm_g if last else None
        xp = ffn_p(xp, norm_g[l, 2], wg[l, 1], wu[l, 1], wd[l, 1], fg)
        xs = ffn_s(xs, norm_g[l, 2], wg[l, 1], wu[l, 1], wd[l, 1], fg)

    return (xp.reshape(n_seq, seq, D_MODEL), xs.reshape(n_dec, dec_seq, D_MODEL),
            jnp.stack(conv_p), jnp.stack(conv_s), jnp.stack(chunk_s),
            jnp.stack(k_p), jnp.stack(v_p), jnp.stack(k_s), jnp.stack(v_s))
```

```python
import functools
import math

import jax
import jax.numpy as jnp
from jax import lax
from jax.experimental import pallas as pl
from jax.experimental.pallas import tpu as pltpu

F32 = jnp.float32
BF16 = jnp.bfloat16

D_MODEL = 1024
D_A = D_MODEL // 2
D_B = D_MODEL // 2
CONV_W = 3
CHUNK = 128
G_B = 4
N_HEADS = 8
HEAD_DIM = 64
N_SUB = 2 * N_HEADS
V_DIM = 2 * HEAD_DIM
D_ATT = N_SUB * HEAD_DIM
ROT_DIM = HEAD_DIM // 4
ROT_HALF = ROT_DIM // 2
ROPE_THETA = 500000.0
D_FF = 2816
EPS = 1e-6
SUBLN_EPS = 1e-5
LN_EPS = 1e-5
NEG_INF = -1e30

LANES = 128
SUBLANES = 8
VMEM_LIMIT = 56 * 1024 * 1024
PAGES_PER_STEP = 16


def _rms(x, g, eps):
    ms = jnp.mean(x * x, axis=-1, keepdims=True)
    return x * lax.rsqrt(ms + eps) * g


def _const_spec(shape):
    return pl.BlockSpec(shape, lambda *_: (0,) * len(shape), pipeline_mode=pl.Buffered(1))


def _params(n_axes):
    return pltpu.CompilerParams(dimension_semantics=("arbitrary",) * n_axes,
                                vmem_limit_bytes=VMEM_LIMIT)


def _ffn_body(x_ref, g_ref, wg_ref, wu_ref, wd_ref, *rest, ff_chunk, has_attn, has_final):
    rest = list(rest)
    o_ref = rest.pop()
    x = x_ref[...]
    if has_attn:
        a_ref, wo_ref = rest[0], rest[1]
        rest = rest[2:]
        x = x + jnp.dot(a_ref[...], wo_ref[...], preferred_element_type=F32)
    h = _rms(x, g_ref[...], EPS).astype(BF16)
    acc = None
    for c in range(D_FF // ff_chunk):
        sl = slice(c * ff_chunk, (c + 1) * ff_chunk)
        gate = jnp.dot(h, wg_ref[:, sl], preferred_element_type=F32)
        up = jnp.dot(h, wu_ref[:, sl], preferred_element_type=F32)
        a = (gate * jax.nn.sigmoid(gate) * up).astype(BF16)
        d = jnp.dot(a, wd_ref[sl, :], preferred_element_type=F32)
        acc = d if acc is None else acc + d
    y = x + 0.5 * acc
    if has_final:
        y = _rms(y, rest[0][...], EPS)
    o_ref[...] = y


def _ffn(x, g, wg, wu, wd, *, tm, ff_chunk, attn=None, w_o=None, final_g=None):
    n = x.shape[0]
    row = pl.BlockSpec((tm, D_MODEL), lambda i: (i, 0))
    in_specs = [row, _const_spec((1, D_MODEL)), _const_spec((D_MODEL, D_FF)),
                _const_spec((D_MODEL, D_FF)), _const_spec((D_FF, D_MODEL))]
    args = [x, g.reshape(1, D_MODEL), wg, wu, wd]
    if attn is not None:
        in_specs += [pl.BlockSpec((tm, D_ATT), lambda i: (i, 0)), _const_spec((D_ATT, D_MODEL))]
        args += [attn, w_o]
    if final_g is not None:
        in_specs.append(_const_spec((1, D_MODEL)))
        args.append(final_g.reshape(1, D_MODEL))
    return pl.pallas_call(
        functools.partial(_ffn_body, ff_chunk=ff_chunk, has_attn=attn is not None,
                          has_final=final_g is not None),
        out_shape=jax.ShapeDtypeStruct((n, D_MODEL), F32),
        grid=(n // tm,), in_specs=in_specs, out_specs=row,
        compiler_params=_params(1), name="ffn")(*args)


def _layer_norm(v, g, b):
    mu = jnp.mean(v, axis=-1, keepdims=True)
    vc = v - mu
    var = jnp.mean(vc * vc, axis=-1, keepdims=True)
    return vc * lax.rsqrt(var + LN_EPS) * g + b


def _mix_prompt_body(x_ref, g_ref, win_ref, cw_ref, lng_ref, lnb_ref, sgw_ref, sgb_ref, wout_ref,
                     o_ref, st_ref, xc_scr, *, tm, tiles_per_seq):
    i = pl.program_id(0)
    x = x_ref[...]
    h = _rms(x, g_ref[...], EPS).astype(BF16)
    proj = jnp.dot(h, win_ref[...], preferred_element_type=F32)
    bg = proj[:, 0:D_A]
    cg = proj[:, D_A:2 * D_A]
    hx = proj[:, 2 * D_A:3 * D_A]
    u = proj[:, 3 * D_A:3 * D_A + D_B]
    v = proj[:, 3 * D_A + D_B:]

    xc = cg * hx

    @pl.when(i % tiles_per_seq == 0)
    def _():
        xc_scr[0:SUBLANES, :] = jnp.zeros((SUBLANES, D_A), F32)

    xc_scr[SUBLANES:SUBLANES + tm, :] = xc
    xm1 = xc_scr[SUBLANES - 1:SUBLANES - 1 + tm, :]
    xm2 = xc_scr[SUBLANES - 2:SUBLANES - 2 + tm, :]
    cw = cw_ref[...]
    ya = bg * (cw[0:1, :] * xm2 + cw[1:2, :] * xm1 + cw[2:3, :] * xc)
    xc_scr[0:SUBLANES, :] = xc[tm - SUBLANES:tm, :]
    st_ref[0] = xc[tm - (CONV_W - 1):tm, :]

    vn = _layer_norm(v, lng_ref[...], lnb_ref[...]).astype(BF16)
    n_chunks = tm // CHUNK
    r_i = lax.broadcasted_iota(jnp.int32, (CHUNK, CHUNK), 0)
    c_i = lax.broadcasted_iota(jnp.int32, (CHUNK, CHUNK), 1)
    zs = []
    for gi in range(G_B):
        w_tril = jnp.where(c_i <= r_i, sgw_ref[gi], 0.0).astype(BF16)
        vg = vn[:, gi * LANES:(gi + 1) * LANES]
        wide = jnp.concatenate([vg[c * CHUNK:(c + 1) * CHUNK, :] for c in range(n_chunks)], axis=1)
        zw = jnp.dot(w_tril, wide, preferred_element_type=F32)
        bias = sgb_ref[gi]
        zs.append(jnp.concatenate(
            [zw[:, c * CHUNK:(c + 1) * CHUNK] + bias for c in range(n_chunks)], axis=0))
    yb = u * jnp.concatenate(zs, axis=1)

    mixin = jnp.concatenate([ya, yb], axis=1).astype(BF16)
    o_ref[...] = x + jnp.dot(mixin, wout_ref[...], preferred_element_type=F32)


def _mix_prompt(x, g, win, cw, lng, lnb, sgw, sgb_tbl, wout, *, n_seq, seq, tm):
    n = x.shape[0]
    tiles_per_seq = seq // tm
    d_in = 3 * D_A + 2 * D_B
    row = pl.BlockSpec((tm, D_MODEL), lambda i: (i, 0))
    in_specs = [row, _const_spec((1, D_MODEL)), _const_spec((D_MODEL, d_in)),
                _const_spec((CONV_W, D_A)), _const_spec((1, D_B)), _const_spec((1, D_B)),
                _const_spec((G_B, CHUNK, CHUNK)), _const_spec((G_B, CHUNK, LANES)),
                _const_spec((D_A + D_B, D_MODEL))]
    out_specs = [row, pl.BlockSpec((1, CONV_W - 1, D_A), lambda i: (i // tiles_per_seq, 0, 0))]
    return pl.pallas_call(
        functools.partial(_mix_prompt_body, tm=tm, tiles_per_seq=tiles_per_seq),
        out_shape=[jax.ShapeDtypeStruct((n, D_MODEL), F32),
                   jax.ShapeDtypeStruct((n_seq, CONV_W - 1, D_A), F32)],
        grid=(n // tm,), in_specs=in_specs, out_specs=out_specs,
        scratch_shapes=[pltpu.VMEM((tm + SUBLANES, D_A), F32)],
        compiler_params=_params(1), name="mix_prompt")(
            x, g.reshape(1, D_MODEL), win, cw, lng.reshape(1, D_B), lnb.reshape(1, D_B),
            sgw, sgb_tbl, wout)


def _mix_decode_body(x_ref, g_ref, win_ref, cw_ref, past_ref, lng_ref, lnb_ref, w00_ref, b0_ref,
                     wout_ref, o_ref, st_ref, vn_ref):
    x = x_ref[...]
    h = _rms(x, g_ref[...], EPS).astype(BF16)
    proj = jnp.dot(h, win_ref[...], preferred_element_type=F32)
    bg = proj[:, 0:D_A]
    cg = proj[:, D_A:2 * D_A]
    hx = proj[:, 2 * D_A:3 * D_A]
    u = proj[:, 3 * D_A:3 * D_A + D_B]
    v = proj[:, 3 * D_A + D_B:]
    xc = cg * hx
    p0 = past_ref[:, 0:D_A]
    p1 = past_ref[:, D_A:2 * D_A]
    cw = cw_ref[...]
    ya = bg * (cw[0:1, :] * p0 + cw[1:2, :] * p1 + cw[2:3, :] * xc)
    st_ref[:, 0:D_A] = p1
    st_ref[:, D_A:2 * D_A] = xc
    vn = _layer_norm(v, lng_ref[...], lnb_ref[...])
    vn_ref[...] = vn
    yb = u * (w00_ref[...] * vn + b0_ref[...])
    mixin = jnp.concatenate([ya, yb], axis=1).astype(BF16)
    o_ref[...] = x + jnp.dot(mixin, wout_ref[...], preferred_element_type=F32)


def _mix_decode(x, g, win, cw, past, lng, lnb, w00, b0, wout):
    n = x.shape[0]
    d_in = 3 * D_A + 2 * D_B
    full = lambda shape: pl.BlockSpec(shape, lambda i: (0,) * len(shape))
    return pl.pallas_call(
        _mix_decode_body,
        out_shape=[jax.ShapeDtypeStruct((n, D_MODEL), F32),
                   jax.ShapeDtypeStruct((n, (CONV_W - 1) * D_A), F32),
                   jax.ShapeDtypeStruct((n, D_B), F32)],
        grid=(1,),
        in_specs=[full((n, D_MODEL)), full((1, D_MODEL)), full((D_MODEL, d_in)),
                  full((CONV_W, D_A)), full((n, (CONV_W - 1) * D_A)), full((1, D_B)),
                  full((1, D_B)), full((1, D_B)), full((1, D_B)), full((D_A + D_B, D_MODEL))],
        out_specs=[full((n, D_MODEL)), full((n, (CONV_W - 1) * D_A)), full((n, D_B))],
        compiler_params=_params(1), name="mix_decode")(
            x, g.reshape(1, D_MODEL), win, cw, past, lng.reshape(1, D_B), lnb.reshape(1, D_B),
            w00, b0, wout)


def _rope_angles(pos):
    inv = ROPE_THETA ** (-(jnp.arange(ROT_HALF, dtype=F32) * 2.0) / ROT_DIM)
    ang = pos.astype(F32)[:, None] * inv[None, :]
    return jnp.cos(ang), jnp.sin(ang)


def _rope_tables(pos):
    cos, sin = _rope_angles(pos)
    pad = jnp.zeros((pos.shape[0], HEAD_DIM - ROT_DIM), F32)
    cos64 = jnp.concatenate([cos, cos, pad + 1.0], axis=1)
    sa64 = jnp.concatenate([jnp.zeros_like(sin), sin, pad], axis=1)
    sb64 = jnp.concatenate([-sin, jnp.zeros_like(sin), pad], axis=1)
    two = lambda a: jnp.concatenate([a, a], axis=1)
    return two(cos64), two(sa64), two(sb64)


def _rope(t, cos, sa, sb):
    blocks = []
    for b in range(t.shape[1] // LANES):
        blk = t[:, b * LANES:(b + 1) * LANES]
        blocks.append(blk * cos + pltpu.roll(blk, ROT_HALF, 1) * sa
                      + pltpu.roll(blk, LANES - ROT_HALF, 1) * sb)
    return jnp.concatenate(blocks, axis=1)


def _qkv_prompt_body(x_ref, g_ref, wq_ref, wkt_ref, wv_ref, cos_ref, sa_ref, sb_ref, cost_ref,
                     sint_ref, q_ref, kt_ref, ktb_ref, v_ref, vb_ref, *, tm):
    h = _rms(x_ref[...], g_ref[...], EPS).astype(BF16)
    q = jnp.dot(h, wq_ref[...], preferred_element_type=F32)
    q = _rope(q, cos_ref[...], sa_ref[...], sb_ref[...]) * (HEAD_DIM ** -0.5)
    q_ref[...] = q.astype(BF16)

    v = jnp.dot(h, wv_ref[...], preferred_element_type=F32)
    vb_ref[...] = v.astype(BF16)
    for hd in range(N_HEADS):
        v_ref[pl.ds(hd, tm, stride=N_HEADS), :] = v[:, hd * V_DIM:(hd + 1) * V_DIM]

    kt = lax.dot_general(wkt_ref[...], h, (((1,), (1,)), ((), ())), preferred_element_type=F32)
    c, s = cost_ref[...], sint_ref[...]
    pieces = []
    for sh in range(N_SUB):
        base = sh * HEAD_DIM
        x1 = kt[base:base + ROT_HALF, :]
        x2 = kt[base + ROT_HALF:base + ROT_DIM, :]
        pieces += [x1 * c - x2 * s, x2 * c + x1 * s, kt[base + ROT_DIM:base + HEAD_DIM, :]]
    kt = jnp.concatenate(pieces, axis=0)
    kt_ref[0] = kt
    ktb_ref[0, 0] = kt.astype(BF16)


def _qkv_prompt(x, g, wq, wkt, wv, tables, tables_t, *, n_seq, seq, tm):
    n = x.shape[0]
    tps = seq // tm
    row = pl.BlockSpec((tm, D_ATT), lambda i: (i, 0))
    tbl = pl.BlockSpec((tm, LANES), lambda i: (i % tps, 0))
    tbl_t = pl.BlockSpec((ROT_HALF, tm), lambda i: (0, i % tps))
    w_spec = _const_spec((D_MODEL, D_ATT))
    return pl.pallas_call(
        functools.partial(_qkv_prompt_body, tm=tm),
        out_shape=[jax.ShapeDtypeStruct((n, D_ATT), BF16),
                   jax.ShapeDtypeStruct((n_seq, D_ATT, seq), F32),
                   jax.ShapeDtypeStruct((n_seq, tps, D_ATT, tm), BF16),
                   jax.ShapeDtypeStruct((n * N_HEADS, V_DIM), F32),
                   jax.ShapeDtypeStruct((n, D_ATT), BF16)],
        grid=(n // tm,),
        in_specs=[pl.BlockSpec((tm, D_MODEL), lambda i: (i, 0)), _const_spec((1, D_MODEL)),
                  w_spec, w_spec, w_spec, tbl, tbl, tbl, tbl_t, tbl_t],
        out_specs=[row,
                   pl.BlockSpec((1, D_ATT, tm), lambda i: (i // tps, 0, i % tps)),
                   pl.BlockSpec((1, 1, D_ATT, tm), lambda i: (i // tps, i % tps, 0, 0)),
                   pl.BlockSpec((tm * N_HEADS, V_DIM), lambda i: (i, 0)),
                   row],
        compiler_params=_params(1), name="qkv_prompt")(
            x, g.reshape(1, D_MODEL), wq, wkt, wv, *tables, *tables_t)


def _qkv_decode_body(x_ref, g_ref, w_ref, cos_ref, sa_ref, sb_ref, q_ref, k_ref, v_ref):
    h = _rms(x_ref[...], g_ref[...], EPS).astype(BF16)
    qkv = jnp.dot(h, w_ref[...], preferred_element_type=F32)
    cos, sa, sb = cos_ref[...], sa_ref[...], sb_ref[...]
    q_ref[...] = _rope(qkv[:, 0:D_ATT], cos, sa, sb) * (HEAD_DIM ** -0.5)
    k_ref[...] = _rope(qkv[:, D_ATT:2 * D_ATT], cos, sa, sb)
    v_ref[...] = qkv[:, 2 * D_ATT:]


def _qkv_decode(x, g, w, tables):
    n = x.shape[0]
    full = lambda shape: pl.BlockSpec(shape, lambda i: (0,) * len(shape))
    out = jax.ShapeDtypeStruct((n, D_ATT), F32)
    return pl.pallas_call(
        _qkv_decode_body, out_shape=[out, out, out], grid=(1,),
        in_specs=[full((n, D_MODEL)), full((1, D_MODEL)), full((D_MODEL, 3 * D_ATT)),
                  full((n, LANES)), full((n, LANES)), full((n, LANES))],
        out_specs=[full((n, D_ATT))] * 3,
        compiler_params=_params(1), name="qkv_decode")(x, g.reshape(1, D_MODEL), w, *tables)


def _lambda(lq, lam_init):
    a = jnp.sum(lq[0:1, :] * lq[1:2, :], axis=-1, keepdims=True)
    b = jnp.sum(lq[2:3, :] * lq[3:4, :], axis=-1, keepdims=True)
    return jnp.exp(a) - jnp.exp(b) + lam_init


def _flash_body(lq_ref, sg_ref, q_ref, kt_ref, v_ref, o_ref, q2_scr, m_scr, l_scr, acc_scr,
                *, tq, rb_full, rb_diag, ck, lam_init):
    qi = pl.program_id(2)
    q = q_ref[0]
    lane = lax.broadcasted_iota(jnp.int32, (tq, V_DIM), 1)
    zero = jnp.zeros_like(q)
    q2_scr[0:tq, :] = jnp.where(lane < HEAD_DIM, q, zero)
    q2_scr[tq:2 * tq, :] = jnp.where(lane >= HEAD_DIM, q, zero)
    m_scr[...] = jnp.full((2 * tq, LANES), NEG_INF, F32)
    l_scr[...] = jnp.zeros((2 * tq, LANES), F32)
    acc_scr[...] = jnp.zeros((2 * tq, V_DIM), F32)

    def step(j, diagonal):
        start = pl.multiple_of(j * tq, tq)
        rb = rb_diag if diagonal else rb_full
        n_rb = 2 * tq // rb

        def n_keys(r):
            return (r * rb) % tq + rb if diagonal else tq

        def scores(r):
            n_k = n_keys(r)
            cols = [kt_ref[0, j * (tq // ck) + c // ck, :, 0:min(ck, n_k - c)]
                    for c in range(0, n_k, ck)]
            kt = cols[0] if len(cols) == 1 else jnp.concatenate(cols, axis=1)
            return jnp.dot(q2_scr[r * rb:(r + 1) * rb, :], kt,
                           preferred_element_type=F32)

        s_next = scores(0)
        for r in range(n_rb):
            rows = slice(r * rb, (r + 1) * rb)
            first = (r * rb) % tq
            n_k = n_keys(r)
            vc = v_ref[0, pl.ds(start, n_k), :]
            s = s_next
            if r + 1 < n_rb:
                s_next = scores(r + 1)
            if diagonal:
                r_i = lax.broadcasted_iota(jnp.int32, (rb, n_k), 0) + first
                c_i = lax.broadcasted_iota(jnp.int32, (rb, n_k), 1)
                s = jnp.where(c_i <= r_i, s, NEG_INF)
            m_prev = m_scr[rows, :]
            m_new = jnp.maximum(m_prev, jnp.max(s, axis=-1, keepdims=True))
            alpha = jnp.exp(m_prev - m_new)
            p = jnp.exp(s - jnp.tile(m_new, (1, n_k // LANES)))
            l_scr[rows, :] = alpha * l_scr[rows, :] + jnp.sum(p, axis=-1, keepdims=True)
            acc_scr[rows, :] = alpha * acc_scr[rows, :] + jnp.dot(
                p.astype(BF16), vc, preferred_element_type=F32)
            m_scr[rows, :] = m_new

    def loop_body(j, carry):
        step(j, False)
        return carry

    lax.fori_loop(0, qi, loop_body, 0)
    step(qi, True)

    o2 = acc_scr[...] / l_scr[...]
    lam = _lambda(lq_ref[...], lam_init)
    o = o2[0:tq, :] - lam * o2[tq:2 * tq, :]
    o = _rms(o, sg_ref[...], SUBLN_EPS) * (1.0 - lam_init)
    o_ref[0] = o.astype(o_ref.dtype)


def _flash(q, ktb, vb, lq, sg, *, n_seq, seq, tq, rb_full, rb_diag, lam_init):
    ck = ktb.shape[3]
    q3 = q.reshape(n_seq, seq, D_ATT)
    v3 = vb.reshape(n_seq, seq, D_ATT)
    q_spec = pl.BlockSpec((1, tq, V_DIM), lambda b, h, i: (b, i, h))
    out = pl.pallas_call(
        functools.partial(_flash_body, tq=tq, rb_full=rb_full, rb_diag=rb_diag, ck=ck,
                          lam_init=lam_init),
        out_shape=jax.ShapeDtypeStruct((n_seq, seq, D_ATT), BF16),
        grid=(n_seq, N_HEADS, seq // tq),
        in_specs=[pl.BlockSpec((4, HEAD_DIM), lambda b, h, i: (0, 0)),
                  pl.BlockSpec((1, V_DIM), lambda b, h, i: (0, 0)), q_spec,
                  pl.BlockSpec((1, seq // ck, V_DIM, ck), lambda b, h, i: (b, 0, h, 0)),
                  pl.BlockSpec((1, seq, V_DIM), lambda b, h, i: (b, 0, h))],
        out_specs=q_spec,
        scratch_shapes=[pltpu.VMEM((2 * tq, V_DIM), BF16), pltpu.VMEM((2 * tq, LANES), F32),
                        pltpu.VMEM((2 * tq, LANES), F32), pltpu.VMEM((2 * tq, V_DIM), F32)],
        compiler_params=_params(3), name="flash_diff")(lq, sg.reshape(1, V_DIM), q3, ktb, v3)
    return out.reshape(n_seq * seq, D_ATT)


def _decode_attn_body(pt_ref, lq_ref, q_ref, kn_ref, vn_ref, *refs, page, lam_init):
    del pt_ref
    k_refs = refs[:PAGES_PER_STEP]
    v_refs = refs[PAGES_PER_STEP:2 * PAGES_PER_STEP]
    o_ref, qx_scr, m_scr, l_scr, acc_scr = refs[2 * PAGES_PER_STEP:]
    j = pl.program_id(1)
    sub = lax.broadcasted_iota(jnp.int32, (N_SUB, D_ATT), 0)
    lane = lax.broadcasted_iota(jnp.int32, (N_SUB, D_ATT), 1)

    @pl.when(j == 0)
    def _():
        qx_scr[...] = jnp.where(lax.shift_right_logical(lane, 6) == sub, q_ref[0], 0.0)
        m_scr[...] = jnp.full((N_SUB, 1), NEG_INF, F32)
        l_scr[...] = jnp.zeros((N_SUB, 1), F32)
        acc_scr[...] = jnp.zeros((N_SUB, D_ATT), F32)

    qx = qx_scr[...].astype(BF16)
    s = jnp.concatenate(
        [jnp.dot(qx, k_refs[r][...].astype(BF16), preferred_element_type=F32)
         for r in range(PAGES_PER_STEP)], axis=1)
    m_prev = m_scr[...]
    m_new = jnp.maximum(m_prev, jnp.max(s, axis=-1, keepdims=True))
    alpha = jnp.exp(m_prev - m_new)
    p = jnp.exp(s - m_new)
    l_scr[...] = alpha * l_scr[...] + jnp.sum(p, axis=-1, keepdims=True)
    pb = p.astype(BF16)
    pv = None
    for r in range(PAGES_PER_STEP):
        v_page = jnp.concatenate(
            [v_refs[r][pl.ds(hd, page, stride=N_HEADS), :] for hd in range(N_HEADS)], axis=1)
        d = jnp.dot(pb[:, r * page:(r + 1) * page], v_page.astype(BF16),
                    preferred_element_type=F32)
        pv = d if pv is None else pv + d
    acc_scr[...] = alpha * acc_scr[...] + pv
    m_scr[...] = m_new

    @pl.when(j == pl.num_programs(1) - 1)
    def _():
        s_new = jnp.sum(qx_scr[...] * kn_ref[0], axis=-1, keepdims=True)
        m_prev = m_scr[...]
        m_fin = jnp.maximum(m_prev, s_new)
        alpha = jnp.exp(m_prev - m_fin)
        p_new = jnp.exp(s_new - m_fin)
        l_fin = alpha * l_scr[...] + p_new
        a = (alpha * acc_scr[...] + p_new * vn_ref[0]) / l_fin
        lam = _lambda(lq_ref[...], lam_init)
        w = jnp.where((sub & 1) == 0, 1.0, -lam)
        own = lax.shift_right_logical(lane, 7) == lax.shift_right_logical(sub, 1)
        o_ref[0] = jnp.sum(jnp.where(own, a * w, 0.0), axis=0, keepdims=True)


def _decode_attn(q, k_new, v_new, cache_k, cache_v, page_table, lq, *, lam_init):
    n = q.shape[0]
    n_phys, page = cache_k.shape[0], cache_k.shape[1]
    n_pages = page_table.shape[1]
    ck = jnp.transpose(cache_k, (0, 2, 3, 1)).reshape(n_phys, D_ATT, page)
    cv = cache_v.reshape(n_phys, page * N_HEADS, V_DIM)
    row = pl.BlockSpec((1, 1, D_ATT), lambda b, j, pt: (b, 0, 0))

    def page_spec(shape, r):
        return pl.BlockSpec((None,) + shape,
                            lambda b, j, pt: (pt[b, j * PAGES_PER_STEP + r], 0, 0))

    grid_spec = pltpu.PrefetchScalarGridSpec(
        num_scalar_prefetch=1, grid=(n, n_pages // PAGES_PER_STEP),
        in_specs=[pl.BlockSpec((4, HEAD_DIM), lambda b, j, pt: (0, 0)), row, row, row]
        + [page_spec((D_ATT, page), r) for r in range(PAGES_PER_STEP)]
        + [page_spec((page * N_HEADS, V_DIM), r) for r in range(PAGES_PER_STEP)],
        out_specs=row,
        scratch_shapes=[pltpu.VMEM((N_SUB, D_ATT), F32), pltpu.VMEM((N_SUB, 1), F32),
                        pltpu.VMEM((N_SUB, 1), F32), pltpu.VMEM((N_SUB, D_ATT), F32)])
    out = pl.pallas_call(
        functools.partial(_decode_attn_body, page=page, lam_init=lam_init),
        out_shape=jax.ShapeDtypeStruct((n, 1, D_ATT), F32), grid_spec=grid_spec,
        compiler_params=_params(2), name="decode_attn")(
            page_table, lq, q.reshape(n, 1, D_ATT), k_new.reshape(n, 1, D_ATT),
            v_new.reshape(n, 1, D_ATT), *([ck] * PAGES_PER_STEP), *([cv] * PAGES_PER_STEP))
    return out.reshape(n, D_ATT)


def _attn_out_decode_body(x_ref, a_ref, w_ref, sg_ref, out_ref, *, lam_init):
    a = a_ref[...]
    sg = sg_ref[...]
    blocks = []
    for hd in range(N_HEADS):
        blk = a[:, hd * V_DIM:(hd + 1) * V_DIM]
        blocks.append(_rms(blk, sg, SUBLN_EPS) * (1.0 - lam_init))
    a = jnp.concatenate(blocks, axis=1).astype(BF16)
    out_ref[...] = x_ref[...] + jnp.dot(a, w_ref[...], preferred_element_type=F32)


def _attn_out_decode(x, a, w, sg, *, lam_init):
    n = x.shape[0]
    full = lambda shape: pl.BlockSpec(shape, lambda i: (0,) * len(shape))
    return pl.pallas_call(
        functools.partial(_attn_out_decode_body, lam_init=lam_init),
        out_shape=jax.ShapeDtypeStruct((n, D_MODEL), F32), grid=(1,),
        in_specs=[full((n, D_MODEL)), full((n, D_ATT)), full((D_ATT, D_MODEL)), full((1, V_DIM))],
        out_specs=full((n, D_MODEL)),
        compiler_params=_params(1), name="attn_out_decode")(x, a, w, sg.reshape(1, V_DIM))


def kernel(x_prompt, x_sample, cache_conv, cache_k, cache_v, page_table, norm_g, ffn_w_gate,
           ffn_w_up, ffn_w_down, w_in_mix, conv_w, sg_ln_g, sg_ln_b, sg_w, sg_b, w_out_mix, w_qkv,
           lambda_qk, subln_g, w_o, final_norm_g):
    n_seq, seq, _ = x_prompt.shape
    n_dec, dec_seq, _ = x_sample.shape
    tm = 512
    assert dec_seq == 1 and seq % tm == 0
    depth = norm_g.shape[0]
    past_len = page_table.shape[1] * cache_k.shape[2]
    assert past_len % CHUNK == 0

    wg, wu, wd = ffn_w_gate.astype(BF16), ffn_w_up.astype(BF16), ffn_w_down.astype(BF16)
    win, wout = w_in_mix.astype(BF16), w_out_mix.astype(BF16)
    wqkv, wo = w_qkv.astype(BF16), w_o.astype(BF16)

    xp = x_prompt.reshape(n_seq * seq, D_MODEL)
    xs = x_sample.reshape(n_dec, D_MODEL)
    ffn_p = functools.partial(_ffn, tm=tm, ff_chunk=256)
    ffn_s = functools.partial(_ffn, tm=n_dec, ff_chunk=D_FF)
    pos_p = jnp.arange(seq)
    pos_s = jnp.broadcast_to(past_len + jnp.arange(dec_seq), (n_dec,))
    tables_p, tables_s = _rope_tables(pos_p), _rope_tables(pos_s)
    tables_pt = tuple(a.T for a in _rope_angles(pos_p))

    conv_p, conv_s, chunk_s, k_p, v_p, k_s, v_s = [], [], [], [], [], [], []
    attn_p = None
    for l in range(depth):
        last = l == depth - 1
        xp = ffn_p(xp, norm_g[l, 0], wg[l, 0], wu[l, 0], wd[l, 0])
        xs = ffn_s(xs, norm_g[l, 0], wg[l, 0], wu[l, 0], wd[l, 0])
        if l % 2 == 0:
            i = l // 2
            sgb_tbl = jnp.broadcast_to(sg_b[i][:, :, None], (G_B, CHUNK, LANES))
            xp, st = _mix_prompt(xp, norm_g[l, 1], win[i], conv_w[i], sg_ln_g[i], sg_ln_b[i],
                                 sg_w[i], sgb_tbl, wout[i], n_seq=n_seq, seq=seq, tm=tm)
            conv_p.append(st)
            w00 = jnp.repeat(sg_w[i][:, 0, 0], D_B // G_B).reshape(1, D_B)
            b0 = jnp.repeat(sg_b[i][:, 0], D_B // G_B).reshape(1, D_B)
            xs, st, vn = _mix_decode(xs, norm_g[l, 1], win[i], conv_w[i],
                                     cache_conv[i].reshape(n_dec, (CONV_W - 1) * D_A),
                                     sg_ln_g[i], sg_ln_b[i], w00, b0, wout[i])
            conv_s.append(st.reshape(n_dec, CONV_W - 1, D_A))
            chunk_s.append(vn.reshape(n_dec, dec_seq, D_B))
        else:
            j = l // 2
            lam_init = 0.8 - 0.6 * math.exp(-0.3 * l)
            wq = wqkv[j][:, 0:D_ATT]
            wkt = wqkv[j][:, D_ATT:2 * D_ATT].T
            wv = wqkv[j][:, 2 * D_ATT:]
            q, kt, ktb, v, vb = _qkv_prompt(xp, norm_g[l, 1], wq, wkt, wv, tables_p, tables_pt,
                                            n_seq=n_seq, seq=seq, tm=tm)
            a = _flash(q, ktb, vb, lambda_qk[j], subln_g[j], n_seq=n_seq, seq=seq, tq=1024,
                       rb_full=512, rb_diag=256, lam_init=lam_init)
            attn_p = (a, wo[j])
            k_p.append(jnp.transpose(kt.reshape(n_seq, N_SUB, HEAD_DIM, seq), (0, 3, 1, 2)))
            v_p.append(v.reshape(n_seq, seq, N_HEADS, V_DIM))

            q, k, v = _qkv_decode(xs, norm_g[l, 1], wqkv[j], tables_s)
            a = _decode_attn(q, k, v, cache_k[j], cache_v[j], page_table, lambda_qk[j],
                             lam_init=lam_init)
            xs = _attn_out_decode(xs, a, wo[j], subln_g[j], lam_init=lam_init)
            k_s.append(k.reshape(n_dec, dec_seq, N_SUB, HEAD_DIM))
            v_s.append(v.reshape(n_dec, dec_seq, N_HEADS, V_DIM))
        fg = final_norm_g if last else None
        if attn_p is None:
            xp = ffn_p(xp, norm_g[l, 2], wg[l, 1], wu[l, 1], wd[l, 1], final_g=fg)
        else:
            xp = ffn_p(xp, norm_g[l, 2], wg[l, 1], wu[l, 1], wd[l, 1], attn=attn_p[0],
                       w_o=attn_p[1], final_g=fg)
            attn_p = None
        xs = ffn_s(xs, norm_g[l, 2], wg[l, 1], wu[l, 1], wd[l, 1], final_g=fg)

    return (xp.reshape(n_seq, seq, D_MODEL), xs.reshape(n_dec, dec_seq, D_MODEL),
            jnp.stack(conv_p), jnp.stack(conv_s), jnp.stack(chunk_s),
            jnp.stack(k_p), jnp.stack(v_p), jnp.stack(k_s), jnp.stack(v_s))
```

```python
import functools
import math

import jax
import jax.numpy as jnp
from jax import lax
from jax.experimental import pallas as pl
from jax.experimental.pallas import tpu as pltpu

F32 = jnp.float32
BF16 = jnp.bfloat16

D_MODEL = 1024
D_A = D_MODEL // 2
D_B = D_MODEL // 2
CONV_W = 3
CHUNK = 128
G_B = 4
N_HEADS = 8
HEAD_DIM = 64
N_SUB = 2 * N_HEADS
V_DIM = 2 * HEAD_DIM
D_ATT = N_SUB * HEAD_DIM
ROT_DIM = HEAD_DIM // 4
ROT_HALF = ROT_DIM // 2
ROPE_THETA = 500000.0
D_FF = 2816
EPS = 1e-6
SUBLN_EPS = 1e-5
LN_EPS = 1e-5
NEG_INF = -1e30

LANES = 128
SUBLANES = 8
VMEM_LIMIT = 56 * 1024 * 1024


def _rms(x, g, eps):
    ms = jnp.mean(x * x, axis=-1, keepdims=True)
    return x * lax.rsqrt(ms + eps) * g


def _const_spec(shape):
    return pl.BlockSpec(shape, lambda *_: (0,) * len(shape), pipeline_mode=pl.Buffered(1))


def _stacked_spec(shape, index):
    return pl.BlockSpec((None,) * len(index) + shape,
                        lambda *_: tuple(index) + (0,) * len(shape),
                        pipeline_mode=pl.Buffered(1))


def _params(n_axes):
    return pltpu.CompilerParams(dimension_semantics=("arbitrary",) * n_axes,
                                vmem_limit_bytes=VMEM_LIMIT)


def _ffn_body(x_ref, g_ref, wg_ref, wu_ref, wd_ref, *rest, ff_chunk, has_attn, has_final):
    rest = list(rest)
    o_ref = rest.pop()
    x = x_ref[...]
    if has_attn:
        a_ref, wo_ref = rest[0], rest[1]
        rest = rest[2:]
        x = x + jnp.dot(a_ref[...], wo_ref[...], preferred_element_type=F32)
    h = _rms(x, g_ref[...], EPS).astype(BF16)
    acc = None
    for c in range(D_FF // ff_chunk):
        sl = slice(c * ff_chunk, (c + 1) * ff_chunk)
        gate = jnp.dot(h, wg_ref[:, sl], preferred_element_type=F32)
        up = jnp.dot(h, wu_ref[:, sl], preferred_element_type=F32)
        a = (gate * jax.nn.sigmoid(gate) * up).astype(BF16)
        d = jnp.dot(a, wd_ref[sl, :], preferred_element_type=F32)
        acc = d if acc is None else acc + d
    y = x + 0.5 * acc
    if has_final:
        y = _rms(y, rest[0][...], EPS)
    o_ref[...] = y


def _ffn(x, g, wg, wu, wd, idx, *, tm, ff_chunk, attn=None, w_o=None, final_g=None):
    n = x.shape[0]
    row = pl.BlockSpec((tm, D_MODEL), lambda i: (i, 0))
    in_specs = [row, _const_spec((1, D_MODEL)), _stacked_spec((D_MODEL, D_FF), idx),
                _stacked_spec((D_MODEL, D_FF), idx), _stacked_spec((D_FF, D_MODEL), idx)]
    args = [x, g.reshape(1, D_MODEL), wg, wu, wd]
    if attn is not None:
        in_specs += [pl.BlockSpec((tm, D_ATT), lambda i: (i, 0)), _const_spec((D_ATT, D_MODEL))]
        args += [attn, w_o]
    if final_g is not None:
        in_specs.append(_const_spec((1, D_MODEL)))
        args.append(final_g.reshape(1, D_MODEL))
    return pl.pallas_call(
        functools.partial(_ffn_body, ff_chunk=ff_chunk, has_attn=attn is not None,
                          has_final=final_g is not None),
        out_shape=jax.ShapeDtypeStruct((n, D_MODEL), F32),
        grid=(n // tm,), in_specs=in_specs, out_specs=row,
        compiler_params=_params(1), name="ffn")(*args)


def _layer_norm(v, g, b):
    mu = jnp.mean(v, axis=-1, keepdims=True)
    vc = v - mu
    var = jnp.mean(vc * vc, axis=-1, keepdims=True)
    return vc * lax.rsqrt(var + LN_EPS) * g + b


def _mix_prompt_body(x_ref, g_ref, win_ref, cw_ref, lng_ref, lnb_ref, sgw_ref, sgb_ref, wout_ref,
                     o_ref, st_ref, xc_scr, *, tm, tiles_per_seq):
    i = pl.program_id(0)
    x = x_ref[...]
    h = _rms(x, g_ref[...], EPS).astype(BF16)
    proj = jnp.dot(h, win_ref[...], preferred_element_type=F32)
    bg = proj[:, 0:D_A]
    cg = proj[:, D_A:2 * D_A]
    hx = proj[:, 2 * D_A:3 * D_A]
    u = proj[:, 3 * D_A:3 * D_A + D_B]
    v = proj[:, 3 * D_A + D_B:]

    xc = cg * hx

    @pl.when(i % tiles_per_seq == 0)
    def _():
        xc_scr[0:SUBLANES, :] = jnp.zeros((SUBLANES, D_A), F32)

    xc_scr[SUBLANES:SUBLANES + tm, :] = xc
    xm1 = xc_scr[SUBLANES - 1:SUBLANES - 1 + tm, :]
    xm2 = xc_scr[SUBLANES - 2:SUBLANES - 2 + tm, :]
    cw = cw_ref[...]
    ya = bg * (cw[0:1, :] * xm2 + cw[1:2, :] * xm1 + cw[2:3, :] * xc)
    xc_scr[0:SUBLANES, :] = xc[tm - SUBLANES:tm, :]
    st_ref[0] = xc[tm - (CONV_W - 1):tm, :]

    vn = _layer_norm(v, lng_ref[...], lnb_ref[...]).astype(BF16)
    n_chunks = tm // CHUNK
    r_i = lax.broadcasted_iota(jnp.int32, (CHUNK, CHUNK), 0)
    c_i = lax.broadcasted_iota(jnp.int32, (CHUNK, CHUNK), 1)
    zs = []
    for gi in range(G_B):
        w_tril = jnp.where(c_i <= r_i, sgw_ref[gi], 0.0).astype(BF16)
        vg = vn[:, gi * LANES:(gi + 1) * LANES]
        wide = jnp.concatenate([vg[c * CHUNK:(c + 1) * CHUNK, :] for c in range(n_chunks)], axis=1)
        zw = jnp.dot(w_tril, wide, preferred_element_type=F32)
        bias = sgb_ref[gi]
        zs.append(jnp.concatenate(
            [zw[:, c * CHUNK:(c + 1) * CHUNK] + bias for c in range(n_chunks)], axis=0))
    yb = u * jnp.concatenate(zs, axis=1)

    mixin = jnp.concatenate([ya, yb], axis=1).astype(BF16)
    o_ref[...] = x + jnp.dot(mixin, wout_ref[...], preferred_element_type=F32)


def _mix_prompt(x, g, win, cw, lng, lnb, sgw, sgb_tbl, wout, *, n_seq, seq, tm):
    n = x.shape[0]
    tiles_per_seq = seq // tm
    d_in = 3 * D_A + 2 * D_B
    row = pl.BlockSpec((tm, D_MODEL), lambda i: (i, 0))
    in_specs = [row, _const_spec((1, D_MODEL)), _const_spec((D_MODEL, d_in)),
                _const_spec((CONV_W, D_A)), _const_spec((1, D_B)), _const_spec((1, D_B)),
                _const_spec((G_B, CHUNK, CHUNK)), _const_spec((G_B, CHUNK, LANES)),
                _const_spec((D_A + D_B, D_MODEL))]
    out_specs = [row, pl.BlockSpec((1, CONV_W - 1, D_A), lambda i: (i // tiles_per_seq, 0, 0))]
    return pl.pallas_call(
        functools.partial(_mix_prompt_body, tm=tm, tiles_per_seq=tiles_per_seq),
        out_shape=[jax.ShapeDtypeStruct((n, D_MODEL), F32),
                   jax.ShapeDtypeStruct((n_seq, CONV_W - 1, D_A), F32)],
        grid=(n // tm,), in_specs=in_specs, out_specs=out_specs,
        scratch_shapes=[pltpu.VMEM((tm + SUBLANES, D_A), F32)],
        compiler_params=_params(1), name="mix_prompt")(
            x, g.reshape(1, D_MODEL), win, cw, lng.reshape(1, D_B), lnb.reshape(1, D_B),
            sgw, sgb_tbl, wout)


def _mix_decode_body(x_ref, g_ref, win_ref, cw_ref, past_ref, lng_ref, lnb_ref, w00_ref, b0_ref,
                     wout_ref, o_ref, st_ref, vn_ref):
    x = x_ref[...]
    h = _rms(x, g_ref[...], EPS).astype(BF16)
    proj = jnp.dot(h, win_ref[...], preferred_element_type=F32)
    bg = proj[:, 0:D_A]
    cg = proj[:, D_A:2 * D_A]
    hx = proj[:, 2 * D_A:3 * D_A]
    u = proj[:, 3 * D_A:3 * D_A + D_B]
    v = proj[:, 3 * D_A + D_B:]
    xc = cg * hx
    p0 = past_ref[:, 0:D_A]
    p1 = past_ref[:, D_A:2 * D_A]
    cw = cw_ref[...]
    ya = bg * (cw[0:1, :] * p0 + cw[1:2, :] * p1 + cw[2:3, :] * xc)
    st_ref[:, 0:D_A] = p1
    st_ref[:, D_A:2 * D_A] = xc
    vn = _layer_norm(v, lng_ref[...], lnb_ref[...])
    vn_ref[...] = vn
    yb = u * (w00_ref[...] * vn + b0_ref[...])
    mixin = jnp.concatenate([ya, yb], axis=1).astype(BF16)
    o_ref[...] = x + jnp.dot(mixin, wout_ref[...], preferred_element_type=F32)


def _mix_decode(x, g, win, cw, past, lng, lnb, w00, b0, wout):
    n = x.shape[0]
    d_in = 3 * D_A + 2 * D_B
    full = lambda shape: pl.BlockSpec(shape, lambda i: (0,) * len(shape))
    return pl.pallas_call(
        _mix_decode_body,
        out_shape=[jax.ShapeDtypeStruct((n, D_MODEL), F32),
                   jax.ShapeDtypeStruct((n, (CONV_W - 1) * D_A), F32),
                   jax.ShapeDtypeStruct((n, D_B), F32)],
        grid=(1,),
        in_specs=[full((n, D_MODEL)), full((1, D_MODEL)), full((D_MODEL, d_in)),
                  full((CONV_W, D_A)), full((n, (CONV_W - 1) * D_A)), full((1, D_B)),
                  full((1, D_B)), full((1, D_B)), full((1, D_B)), full((D_A + D_B, D_MODEL))],
        out_specs=[full((n, D_MODEL)), full((n, (CONV_W - 1) * D_A)), full((n, D_B))],
        compiler_params=_params(1), name="mix_decode")(
            x, g.reshape(1, D_MODEL), win, cw, past, lng.reshape(1, D_B), lnb.reshape(1, D_B),
            w00, b0, wout)


def _rope_angles(pos):
    inv = ROPE_THETA ** (-(jnp.arange(ROT_HALF, dtype=F32) * 2.0) / ROT_DIM)
    ang = pos.astype(F32)[:, None] * inv[None, :]
    return jnp.cos(ang), jnp.sin(ang)


def _rope_tables(pos):
    cos, sin = _rope_angles(pos)
    pad = jnp.zeros((pos.shape[0], HEAD_DIM - ROT_DIM), F32)
    cos64 = jnp.concatenate([cos, cos, pad + 1.0], axis=1)
    sa64 = jnp.concatenate([jnp.zeros_like(sin), sin, pad], axis=1)
    sb64 = jnp.concatenate([-sin, jnp.zeros_like(sin), pad], axis=1)
    two = lambda a: jnp.concatenate([a, a], axis=1)
    return two(cos64), two(sa64), two(sb64)


def _rope(t, cos, sa, sb):
    blocks = []
    for b in range(t.shape[1] // LANES):
        blk = t[:, b * LANES:(b + 1) * LANES]
        blocks.append(blk * cos + pltpu.roll(blk, ROT_HALF, 1) * sa
                      + pltpu.roll(blk, LANES - ROT_HALF, 1) * sb)
    return jnp.concatenate(blocks, axis=1)


def _qkv_prompt_body(x_ref, g_ref, wq_ref, wkt_ref, wv_ref, cos_ref, sa_ref, sb_ref, cost_ref,
                     sint_ref, q_ref, kt_ref, ktb_ref, v_ref, vb_ref, *, tm):
    h = _rms(x_ref[...], g_ref[...], EPS).astype(BF16)
    q = jnp.dot(h, wq_ref[...], preferred_element_type=F32)
    q = _rope(q, cos_ref[...], sa_ref[...], sb_ref[...]) * (HEAD_DIM ** -0.5)
    q_ref[...] = q.astype(BF16)

    v = jnp.dot(h, wv_ref[...], preferred_element_type=F32)
    vb_ref[...] = v.astype(BF16)
    for hd in range(N_HEADS):
        v_ref[pl.ds(hd, tm, stride=N_HEADS), :] = v[:, hd * V_DIM:(hd + 1) * V_DIM]

    kt = lax.dot_general(wkt_ref[...], h, (((1,), (1,)), ((), ())), preferred_element_type=F32)
    c, s = cost_ref[...], sint_ref[...]
    pieces = []
    for sh in range(N_SUB):
        base = sh * HEAD_DIM
        x1 = kt[base:base + ROT_HALF, :]
        x2 = kt[base + ROT_HALF:base + ROT_DIM, :]
        pieces += [x1 * c - x2 * s, x2 * c + x1 * s, kt[base + ROT_DIM:base + HEAD_DIM, :]]
    kt = jnp.concatenate(pieces, axis=0)
    kt_ref[0] = kt
    ktb_ref[0, 0] = kt.astype(BF16)


def _qkv_prompt(x, g, wq, wkt, wv, tables, tables_t, *, n_seq, seq, tm):
    n = x.shape[0]
    tps = seq // tm
    row = pl.BlockSpec((tm, D_ATT), lambda i: (i, 0))
    tbl = pl.BlockSpec((tm, LANES), lambda i: (i % tps, 0))
    tbl_t = pl.BlockSpec((ROT_HALF, tm), lambda i: (0, i % tps))
    w_spec = _const_spec((D_MODEL, D_ATT))
    return pl.pallas_call(
        functools.partial(_qkv_prompt_body, tm=tm),
        out_shape=[jax.ShapeDtypeStruct((n, D_ATT), BF16),
                   jax.ShapeDtypeStruct((n_seq, D_ATT, seq), F32),
                   jax.ShapeDtypeStruct((n_seq, tps, D_ATT, tm), BF16),
                   jax.ShapeDtypeStruct((n * N_HEADS, V_DIM), F32),
                   jax.ShapeDtypeStruct((n, D_ATT), BF16)],
        grid=(n // tm,),
        in_specs=[pl.BlockSpec((tm, D_MODEL), lambda i: (i, 0)), _const_spec((1, D_MODEL)),
                  w_spec, w_spec, w_spec, tbl, tbl, tbl, tbl_t, tbl_t],
        out_specs=[row,
                   pl.BlockSpec((1, D_ATT, tm), lambda i: (i // tps, 0, i % tps)),
                   pl.BlockSpec((1, 1, D_ATT, tm), lambda i: (i // tps, i % tps, 0, 0)),
                   pl.BlockSpec((tm * N_HEADS, V_DIM), lambda i: (i, 0)),
                   row],
        compiler_params=_params(1), name="qkv_prompt")(
            x, g.reshape(1, D_MODEL), wq, wkt, wv, *tables, *tables_t)


def _qkv_decode_body(x_ref, g_ref, w_ref, cos_ref, sa_ref, sb_ref, q_ref, k_ref, v_ref):
    h = _rms(x_ref[...], g_ref[...], EPS).astype(BF16)
    qkv = jnp.dot(h, w_ref[...], preferred_element_type=F32)
    cos, sa, sb = cos_ref[...], sa_ref[...], sb_ref[...]
    q_ref[...] = _rope(qkv[:, 0:D_ATT], cos, sa, sb) * (HEAD_DIM ** -0.5)
    k_ref[...] = _rope(qkv[:, D_ATT:2 * D_ATT], cos, sa, sb)
    v_ref[...] = qkv[:, 2 * D_ATT:]


def _qkv_decode(x, g, w, tables):
    n = x.shape[0]
    full = lambda shape: pl.BlockSpec(shape, lambda i: (0,) * len(shape))
    out = jax.ShapeDtypeStruct((n, D_ATT), F32)
    return pl.pallas_call(
        _qkv_decode_body, out_shape=[out, out, out], grid=(1,),
        in_specs=[full((n, D_MODEL)), full((1, D_MODEL)), full((D_MODEL, 3 * D_ATT)),
                  full((n, LANES)), full((n, LANES)), full((n, LANES))],
        out_specs=[full((n, D_ATT))] * 3,
        compiler_params=_params(1), name="qkv_decode")(x, g.reshape(1, D_MODEL), w, *tables)


def _lambda(lq, lam_init):
    a = jnp.sum(lq[0:1, :] * lq[1:2, :], axis=-1, keepdims=True)
    b = jnp.sum(lq[2:3, :] * lq[3:4, :], axis=-1, keepdims=True)
    return jnp.exp(a) - jnp.exp(b) + lam_init


def _attn_body(pt_ref, lq_ref, sg_ref, q_ref, kt_ref, v_ref, qd_ref, kn_ref, vn_ref, *refs,
               tq, rb_full, rb_diag, ck, page, pages, dec_steps, lam_init):
    del pt_ref
    k_refs = refs[:pages]
    v_refs = refs[pages:2 * pages]
    (o_ref, od_ref, q2_scr, m_scr, l_scr, acc_scr,
     qx_scr, sd_scr, pd_scr, md_scr, ld_scr, accd_scr) = refs[2 * pages:]
    qi = pl.program_id(2)
    step_id = (pl.program_id(0) * pl.num_programs(1) + pl.program_id(1)) * pl.num_programs(2) + qi
    jd = lax.rem(step_id, dec_steps)
    sub = lax.broadcasted_iota(jnp.int32, (N_SUB, D_ATT), 0)
    lane_d = lax.broadcasted_iota(jnp.int32, (N_SUB, D_ATT), 1)

    def q_expanded():
        return jnp.where(lax.shift_right_logical(lane_d, 6) == sub, qd_ref[0], 0.0)

    @pl.when(jd == 0)
    def _():
        qx_scr[...] = q_expanded().astype(BF16)
        md_scr[...] = jnp.full((N_SUB, 1), NEG_INF, F32)
        ld_scr[...] = jnp.zeros((N_SUB, 1), F32)
        accd_scr[...] = jnp.zeros((N_SUB, D_ATT), F32)

    def decode_scores(r):
        sd_scr[:, r * page:(r + 1) * page] = jnp.dot(
            qx_scr[...], k_refs[r][...].astype(BF16), preferred_element_type=F32)

    def decode_softmax():
        s = sd_scr[...]
        m_prev = md_scr[...]
        m_new = jnp.maximum(m_prev, jnp.max(s, axis=-1, keepdims=True))
        alpha = jnp.exp(m_prev - m_new)
        p = jnp.exp(s - m_new)
        ld_scr[...] = alpha * ld_scr[...] + jnp.sum(p, axis=-1, keepdims=True)
        pd_scr[...] = p.astype(BF16)
        accd_scr[...] = alpha * accd_scr[...]
        md_scr[...] = m_new

    def decode_values(r):
        v_page = jnp.concatenate(
            [v_refs[r][pl.ds(hd, page, stride=N_HEADS), :] for hd in range(N_HEADS)], axis=1)
        accd_scr[...] += jnp.dot(pd_scr[:, r * page:(r + 1) * page], v_page.astype(BF16),
                                 preferred_element_type=F32)

    n_rb_diag = 2 * tq // rb_diag
    half = n_rb_diag // 2
    per_block = pages // half
    pieces = [[] for _ in range(n_rb_diag)]
    for r in range(pages):
        pieces[r // per_block].append(functools.partial(decode_scores, r))
        pieces[half + r // per_block].append(functools.partial(decode_values, r))
    pieces[half - 1].append(decode_softmax)

    q = q_ref[0]
    lane = lax.broadcasted_iota(jnp.int32, (tq, V_DIM), 1)
    zero = jnp.zeros_like(q)
    q2_scr[0:tq, :] = jnp.where(lane < HEAD_DIM, q, zero)
    q2_scr[tq:2 * tq, :] = jnp.where(lane >= HEAD_DIM, q, zero)
    m_scr[...] = jnp.full((2 * tq, LANES), NEG_INF, F32)
    l_scr[...] = jnp.zeros((2 * tq, LANES), F32)
    acc_scr[...] = jnp.zeros((2 * tq, V_DIM), F32)

    def step(j, diagonal):
        start = pl.multiple_of(j * tq, tq)
        rb = rb_diag if diagonal else rb_full
        n_rb = 2 * tq // rb

        def n_keys(r):
            return (r * rb) % tq + rb if diagonal else tq

        def scores(r):
            n_k = n_keys(r)
            cols = [kt_ref[0, j * (tq // ck) + c // ck, :, 0:min(ck, n_k - c)]
                    for c in range(0, n_k, ck)]
            kt = cols[0] if len(cols) == 1 else jnp.concatenate(cols, axis=1)
            return jnp.dot(q2_scr[r * rb:(r + 1) * rb, :], kt,
                           preferred_element_type=F32)

        s_next = scores(0)
        for r in range(n_rb):
            rows = slice(r * rb, (r + 1) * rb)
            first = (r * rb) % tq
            n_k = n_keys(r)
            vc = v_ref[0, pl.ds(start, n_k), :]
            s = s_next
            if r + 1 < n_rb:
                s_next = scores(r + 1)
            if diagonal:
                r_i = lax.broadcasted_iota(jnp.int32, (rb, n_k), 0) + first
                c_i = lax.broadcasted_iota(jnp.int32, (rb, n_k), 1)
                s = jnp.where(c_i <= r_i, s, NEG_INF)
            m_prev = m_scr[rows, :]
            m_new = jnp.maximum(m_prev, jnp.max(s, axis=-1, keepdims=True))
            alpha = jnp.exp(m_prev - m_new)
            p = jnp.exp(s - jnp.tile(m_new, (1, n_k // LANES)))
            l_scr[rows, :] = alpha * l_scr[rows, :] + jnp.sum(p, axis=-1, keepdims=True)
            acc_scr[rows, :] = alpha * acc_scr[rows, :] + jnp.dot(
                p.astype(BF16), vc, preferred_element_type=F32)
            m_scr[rows, :] = m_new
            if diagonal:
                for piece in pieces[r]:
                    piece()

    def loop_body(j, carry):
        step(j, False)
        return carry

    lax.fori_loop(0, qi, loop_body, 0)
    step(qi, True)

    o2 = acc_scr[...] / l_scr[...]
    lam = _lambda(lq_ref[...], lam_init)
    o = o2[0:tq, :] - lam * o2[tq:2 * tq, :]
    o = _rms(o, sg_ref[...], SUBLN_EPS) * (1.0 - lam_init)
    o_ref[0] = o.astype(o_ref.dtype)

    @pl.when(jd == dec_steps - 1)
    def _():
        s_new = jnp.sum(q_expanded() * kn_ref[0], axis=-1, keepdims=True)
        m_prev = md_scr[...]
        m_fin = jnp.maximum(m_prev, s_new)
        alpha = jnp.exp(m_prev - m_fin)
        p_new = jnp.exp(s_new - m_fin)
        l_fin = alpha * ld_scr[...] + p_new
        a = (alpha * accd_scr[...] + p_new * vn_ref[0]) / l_fin
        w = jnp.where((sub & 1) == 0, 1.0, -lam)
        own = lax.shift_right_logical(lane_d, 7) == lax.shift_right_logical(sub, 1)
        od_ref[0] = jnp.sum(jnp.where(own, a * w, 0.0), axis=0, keepdims=True)


def _attention(q, ktb, vb, qd, kd, vd, cache_k, cache_v, page_table, lq, sg, *, n_seq, seq, tq,
               rb_full, rb_diag, lam_init):
    ck = ktb.shape[3]
    n_dec = qd.shape[0]
    n_phys, page = cache_k.shape[0], cache_k.shape[1]
    n_pages = page_table.shape[1]
    n_q = seq // tq
    n_steps = n_seq * N_HEADS * n_q
    dec_steps = n_steps // n_dec
    pages = n_pages // dec_steps
    assert dec_steps * n_dec == n_steps and pages * dec_steps == n_pages
    assert pages % (tq // rb_diag) == 0
    q3 = q.reshape(n_seq, seq, D_ATT)
    v3 = vb.reshape(n_seq, seq, D_ATT)
    c_k = jnp.transpose(cache_k, (0, 2, 3, 1)).reshape(n_phys, D_ATT, page)
    c_v = cache_v.reshape(n_phys, page * N_HEADS, V_DIM)

    def lin(b, h, i):
        return (b * N_HEADS + h) * n_q + i

    q_spec = pl.BlockSpec((1, tq, V_DIM), lambda b, h, i, pt: (b, i, h))
    d_spec = pl.BlockSpec((1, 1, D_ATT), lambda b, h, i, pt: (lin(b, h, i) // dec_steps, 0, 0))

    def page_spec(shape, r):
        def index(b, h, i, pt):
            t = lin(b, h, i)
            return (pt[t // dec_steps, (t % dec_steps) * pages + r], 0, 0)
        return pl.BlockSpec((None,) + shape, index)

    grid_spec = pltpu.PrefetchScalarGridSpec(
        num_scalar_prefetch=1, grid=(n_seq, N_HEADS, n_q),
        in_specs=[pl.BlockSpec((4, HEAD_DIM), lambda b, h, i, pt: (0, 0)),
                  pl.BlockSpec((1, V_DIM), lambda b, h, i, pt: (0, 0)), q_spec,
                  pl.BlockSpec((1, seq // ck, V_DIM, ck), lambda b, h, i, pt: (b, 0, h, 0)),
                  pl.BlockSpec((1, seq, V_DIM), lambda b, h, i, pt: (b, 0, h)),
                  d_spec, d_spec, d_spec]
        + [page_spec((D_ATT, page), r) for r in range(pages)]
        + [page_spec((page * N_HEADS, V_DIM), r) for r in range(pages)],
        out_specs=[q_spec, d_spec],
        scratch_shapes=[pltpu.VMEM((2 * tq, V_DIM), BF16), pltpu.VMEM((2 * tq, LANES), F32),
                        pltpu.VMEM((2 * tq, LANES), F32), pltpu.VMEM((2 * tq, V_DIM), F32),
                        pltpu.VMEM((N_SUB, D_ATT), BF16), pltpu.VMEM((N_SUB, pages * page), F32),
                        pltpu.VMEM((N_SUB, pages * page), BF16), pltpu.VMEM((N_SUB, 1), F32),
                        pltpu.VMEM((N_SUB, 1), F32), pltpu.VMEM((N_SUB, D_ATT), F32)])
    out, out_d = pl.pallas_call(
        functools.partial(_attn_body, tq=tq, rb_full=rb_full, rb_diag=rb_diag, ck=ck, page=page,
                          pages=pages, dec_steps=dec_steps, lam_init=lam_init),
        out_shape=[jax.ShapeDtypeStruct((n_seq, seq, D_ATT), BF16),
                   jax.ShapeDtypeStruct((n_dec, 1, D_ATT), F32)],
        grid_spec=grid_spec, compiler_params=_params(3), name="attention")(
            page_table, lq, sg.reshape(1, V_DIM), q3, ktb, v3, qd.reshape(n_dec, 1, D_ATT),
            kd.reshape(n_dec, 1, D_ATT), vd.reshape(n_dec, 1, D_ATT),
            *([c_k] * pages), *([c_v] * pages))
    return out.reshape(n_seq * seq, D_ATT), out_d.reshape(n_dec, D_ATT)


def _attn_out_decode_body(x_ref, a_ref, w_ref, sg_ref, out_ref, *, lam_init):
    a = a_ref[...]
    sg = sg_ref[...]
    blocks = []
    for hd in range(N_HEADS):
        blk = a[:, hd * V_DIM:(hd + 1) * V_DIM]
        blocks.append(_rms(blk, sg, SUBLN_EPS) * (1.0 - lam_init))
    a = jnp.concatenate(blocks, axis=1).astype(BF16)
    out_ref[...] = x_ref[...] + jnp.dot(a, w_ref[...], preferred_element_type=F32)


def _attn_out_decode(x, a, w, sg, *, lam_init):
    n = x.shape[0]
    full = lambda shape: pl.BlockSpec(shape, lambda i: (0,) * len(shape))
    return pl.pallas_call(
        functools.partial(_attn_out_decode_body, lam_init=lam_init),
        out_shape=jax.ShapeDtypeStruct((n, D_MODEL), F32), grid=(1,),
        in_specs=[full((n, D_MODEL)), full((n, D_ATT)), full((D_ATT, D_MODEL)), full((1, V_DIM))],
        out_specs=full((n, D_MODEL)),
        compiler_params=_params(1), name="attn_out_decode")(x, a, w, sg.reshape(1, V_DIM))


def kernel(x_prompt, x_sample, cache_conv, cache_k, cache_v, page_table, norm_g, ffn_w_gate,
           ffn_w_up, ffn_w_down, w_in_mix, conv_w, sg_ln_g, sg_ln_b, sg_w, sg_b, w_out_mix, w_qkv,
           lambda_qk, subln_g, w_o, final_norm_g):
    n_seq, seq, _ = x_prompt.shape
    n_dec, dec_seq, _ = x_sample.shape
    tm = 512
    assert dec_seq == 1 and seq % tm == 0
    depth = norm_g.shape[0]
    past_len = page_table.shape[1] * cache_k.shape[2]
    assert past_len % CHUNK == 0

    wg, wu, wd = ffn_w_gate.astype(BF16), ffn_w_up.astype(BF16), ffn_w_down.astype(BF16)
    win, wout = w_in_mix.astype(BF16), w_out_mix.astype(BF16)
    wqkv, wo = w_qkv.astype(BF16), w_o.astype(BF16)

    xp = x_prompt.reshape(n_seq * seq, D_MODEL)
    xs = x_sample.reshape(n_dec, D_MODEL)
    ffn_p = functools.partial(_ffn, tm=tm, ff_chunk=256)
    ffn_s = functools.partial(_ffn, tm=n_dec, ff_chunk=D_FF)
    pos_p = jnp.arange(seq)
    pos_s = jnp.broadcast_to(past_len + jnp.arange(dec_seq), (n_dec,))
    tables_p, tables_s = _rope_tables(pos_p), _rope_tables(pos_s)
    tables_pt = tuple(a.T for a in _rope_angles(pos_p))

    conv_p, conv_s, chunk_s, k_p, v_p, k_s, v_s = [], [], [], [], [], [], []
    attn_p = None
    for l in range(depth):
        last = l == depth - 1
        xp = ffn_p(xp, norm_g[l, 0], wg, wu, wd, (l, 0))
        xs = ffn_s(xs, norm_g[l, 0], wg, wu, wd, (l, 0))
        if l % 2 == 0:
            i = l // 2
            sgb_tbl = jnp.broadcast_to(sg_b[i][:, :, None], (G_B, CHUNK, LANES))
            xp, st = _mix_prompt(xp, norm_g[l, 1], win[i], conv_w[i], sg_ln_g[i], sg_ln_b[i],
                                 sg_w[i], sgb_tbl, wout[i], n_seq=n_seq, seq=seq, tm=tm)
            conv_p.append(st)
            w00 = jnp.repeat(sg_w[i][:, 0, 0], D_B // G_B).reshape(1, D_B)
            b0 = jnp.repeat(sg_b[i][:, 0], D_B // G_B).reshape(1, D_B)
            xs, st, vn = _mix_decode(xs, norm_g[l, 1], win[i], conv_w[i],
                                     cache_conv[i].reshape(n_dec, (CONV_W - 1) * D_A),
                                     sg_ln_g[i], sg_ln_b[i], w00, b0, wout[i])
            conv_s.append(st.reshape(n_dec, CONV_W - 1, D_A))
            chunk_s.append(vn.reshape(n_dec, dec_seq, D_B))
        else:
            j = l // 2
            lam_init = 0.8 - 0.6 * math.exp(-0.3 * l)
            wq = wqkv[j][:, 0:D_ATT]
            wkt = wqkv[j][:, D_ATT:2 * D_ATT].T
            wv = wqkv[j][:, 2 * D_ATT:]
            q, kt, ktb, v, vb = _qkv_prompt(xp, norm_g[l, 1], wq, wkt, wv, tables_p, tables_pt,
                                            n_seq=n_seq, seq=seq, tm=tm)
            qd, kd, vd = _qkv_decode(xs, norm_g[l, 1], wqkv[j], tables_s)
            a, ad = _attention(q, ktb, vb, qd, kd, vd, cache_k[j], cache_v[j], page_table,
                               lambda_qk[j], subln_g[j], n_seq=n_seq, seq=seq, tq=1024,
                               rb_full=512, rb_diag=256, lam_init=lam_init)
            attn_p = (a, wo[j])
            xs = _attn_out_decode(xs, ad, wo[j], subln_g[j], lam_init=lam_init)
            k_p.append(jnp.transpose(kt.reshape(n_seq, N_SUB, HEAD_DIM, seq), (0, 3, 1, 2)))
            v_p.append(v.reshape(n_seq, seq, N_HEADS, V_DIM))
            k_s.append(kd.reshape(n_dec, dec_seq, N_SUB, HEAD_DIM))
            v_s.append(vd.reshape(n_dec, dec_seq, N_HEADS, V_DIM))
        fg = final_norm_g if last else None
        if attn_p is None:
            xp = ffn_p(xp, norm_g[l, 2], wg, wu, wd, (l, 1), final_g=fg)
        else:
            xp = ffn_p(xp, norm_g[l, 2], wg, wu, wd, (l, 1), attn=attn_p[0], w_o=attn_p[1],
                       final_g=fg)
            attn_p = None
        xs = ffn_s(xs, norm_g[l, 2], wg, wu, wd, (l, 1), final_g=fg)

    return (xp.reshape(n_seq, seq, D_MODEL), xs.reshape(n_dec, dec_seq, D_MODEL),
            jnp.stack(conv_p), jnp.stack(conv_s), jnp.stack(chunk_s),
            jnp.stack(k_p), jnp.stack(v_p), jnp.stack(k_s), jnp.stack(v_s))
```

```python
import functools
import math

import jax
import jax.numpy as jnp
from jax import lax
from jax.experimental import pallas as pl
from jax.experimental.pallas import tpu as pltpu

F32 = jnp.float32
BF16 = jnp.bfloat16

D_MODEL = 1024
D_A = D_MODEL // 2
D_B = D_MODEL // 2
CONV_W = 3
CHUNK = 128
G_B = 4
N_HEADS = 8
HEAD_DIM = 64
N_SUB = 2 * N_HEADS
V_DIM = 2 * HEAD_DIM
D_ATT = N_SUB * HEAD_DIM
ROT_DIM = HEAD_DIM // 4
ROT_HALF = ROT_DIM // 2
ROPE_THETA = 500000.0
D_FF = 2816
EPS = 1e-6
SUBLN_EPS = 1e-5
LN_EPS = 1e-5
NEG_INF = -1e30
LOG2_E = math.log2(math.e)

LANES = 128
SUBLANES = 8
VMEM_LIMIT = 56 * 1024 * 1024
QK_AHEAD = 1


def _rms(x, g, eps):
    ms = jnp.mean(x * x, axis=-1, keepdims=True)
    return x * lax.rsqrt(ms + eps) * g


def _const_spec(shape):
    return pl.BlockSpec(shape, lambda *_: (0,) * len(shape), pipeline_mode=pl.Buffered(1))


def _stacked_spec(shape, index):
    return pl.BlockSpec((None,) * len(index) + shape,
                        lambda *_: tuple(index) + (0,) * len(shape),
                        pipeline_mode=pl.Buffered(1))


def _params(n_axes):
    return pltpu.CompilerParams(dimension_semantics=("arbitrary",) * n_axes,
                                vmem_limit_bytes=VMEM_LIMIT)


def _ffn_body(x_ref, g_ref, wg_ref, wu_ref, wd_ref, *rest, ff_chunk, has_attn, has_final):
    rest = list(rest)
    o_ref = rest.pop()
    x = x_ref[...]
    if has_attn:
        a_ref, wo_ref = rest[0], rest[1]
        rest = rest[2:]
        x = x + jnp.dot(a_ref[...], wo_ref[...], preferred_element_type=F32)
    h = _rms(x, g_ref[...], EPS).astype(BF16)
    acc = None
    for c in range(D_FF // ff_chunk):
        sl = slice(c * ff_chunk, (c + 1) * ff_chunk)
        gate = jnp.dot(h, wg_ref[:, sl], preferred_element_type=F32)
        up = jnp.dot(h, wu_ref[:, sl], preferred_element_type=F32)
        a = (gate * jax.nn.sigmoid(gate) * up).astype(BF16)
        d = jnp.dot(a, wd_ref[sl, :], preferred_element_type=F32)
        acc = d if acc is None else acc + d
    y = x + 0.5 * acc
    if has_final:
        y = _rms(y, rest[0][...], EPS)
    o_ref[...] = y


def _ffn(x, g, wg, wu, wd, idx, *, tm, ff_chunk, attn=None, w_o=None, final_g=None):
    n = x.shape[0]
    row = pl.BlockSpec((tm, D_MODEL), lambda i: (i, 0))
    in_specs = [row, _const_spec((1, D_MODEL)), _stacked_spec((D_MODEL, D_FF), idx),
                _stacked_spec((D_MODEL, D_FF), idx), _stacked_spec((D_FF, D_MODEL), idx)]
    args = [x, g.reshape(1, D_MODEL), wg, wu, wd]
    if attn is not None:
        in_specs += [pl.BlockSpec((tm, D_ATT), lambda i: (i, 0)), _const_spec((D_ATT, D_MODEL))]
        args += [attn, w_o]
    if final_g is not None:
        in_specs.append(_const_spec((1, D_MODEL)))
        args.append(final_g.reshape(1, D_MODEL))
    return pl.pallas_call(
        functools.partial(_ffn_body, ff_chunk=ff_chunk, has_attn=attn is not None,
                          has_final=final_g is not None),
        out_shape=jax.ShapeDtypeStruct((n, D_MODEL), F32),
        grid=(n // tm,), in_specs=in_specs, out_specs=row,
        compiler_params=_params(1), name="ffn")(*args)


def _layer_norm(v, g, b):
    mu = jnp.mean(v, axis=-1, keepdims=True)
    vc = v - mu
    var = jnp.mean(vc * vc, axis=-1, keepdims=True)
    return vc * lax.rsqrt(var + LN_EPS) * g + b


def _mix_prompt_body(x_ref, g_ref, win_ref, cw_ref, lng_ref, lnb_ref, sgw_ref, sgb_ref, wout_ref,
                     o_ref, st_ref, xc_scr, *, tm, tiles_per_seq, parts):
    i = pl.program_id(0)
    pm = tm // parts
    n_chunks = pm // CHUNK
    r_i = lax.broadcasted_iota(jnp.int32, (CHUNK, CHUNK), 0)
    c_i = lax.broadcasted_iota(jnp.int32, (CHUNK, CHUNK), 1)
    w_tril = [jnp.where(c_i <= r_i, sgw_ref[gi], 0.0).astype(BF16) for gi in range(G_B)]
    cw = cw_ref[...]

    def in_proj(k):
        h = _rms(x_ref[k * pm:(k + 1) * pm, :], g_ref[...], EPS).astype(BF16)
        return jnp.dot(h, win_ref[...], preferred_element_type=F32)

    @pl.when(i % tiles_per_seq == 0)
    def _():
        xc_scr[0:SUBLANES, :] = jnp.zeros((SUBLANES, D_A), F32)

    proj_next = in_proj(0)
    for k in range(parts):
        proj = proj_next
        if k + 1 < parts:
            proj_next = in_proj(k + 1)
        bg = proj[:, 0:D_A]
        cg = proj[:, D_A:2 * D_A]
        hx = proj[:, 2 * D_A:3 * D_A]
        u = proj[:, 3 * D_A:3 * D_A + D_B]
        v = proj[:, 3 * D_A + D_B:]

        xc = cg * hx
        base = SUBLANES + k * pm
        xc_scr[base:base + pm, :] = xc
        xm1 = xc_scr[base - 1:base - 1 + pm, :]
        xm2 = xc_scr[base - 2:base - 2 + pm, :]
        ya = bg * (cw[0:1, :] * xm2 + cw[1:2, :] * xm1 + cw[2:3, :] * xc)
        if k == parts - 1:
            st_ref[0] = xc[pm - (CONV_W - 1):pm, :]

        vn = _layer_norm(v, lng_ref[...], lnb_ref[...]).astype(BF16)
        zs = []
        for gi in range(G_B):
            vg = vn[:, gi * LANES:(gi + 1) * LANES]
            wide = jnp.concatenate([vg[c * CHUNK:(c + 1) * CHUNK, :] for c in range(n_chunks)],
                                   axis=1)
            zw = jnp.dot(w_tril[gi], wide, preferred_element_type=F32)
            bias = sgb_ref[gi]
            zs.append(jnp.concatenate(
                [zw[:, c * CHUNK:(c + 1) * CHUNK] + bias for c in range(n_chunks)], axis=0))
        yb = u * jnp.concatenate(zs, axis=1)

        mixin = jnp.concatenate([ya, yb], axis=1).astype(BF16)
        o_ref[k * pm:(k + 1) * pm, :] = x_ref[k * pm:(k + 1) * pm, :] + jnp.dot(
            mixin, wout_ref[...], preferred_element_type=F32)

    xc_scr[0:SUBLANES, :] = xc_scr[tm:tm + SUBLANES, :]


def _mix_prompt(x, g, win, cw, lng, lnb, sgw, sgb_tbl, wout, *, n_seq, seq, tm):
    n = x.shape[0]
    tiles_per_seq = seq // tm
    d_in = 3 * D_A + 2 * D_B
    row = pl.BlockSpec((tm, D_MODEL), lambda i: (i, 0))
    in_specs = [row, _const_spec((1, D_MODEL)), _const_spec((D_MODEL, d_in)),
                _const_spec((CONV_W, D_A)), _const_spec((1, D_B)), _const_spec((1, D_B)),
                _const_spec((G_B, CHUNK, CHUNK)), _const_spec((G_B, CHUNK, LANES)),
                _const_spec((D_A + D_B, D_MODEL))]
    out_specs = [row, pl.BlockSpec((1, CONV_W - 1, D_A), lambda i: (i // tiles_per_seq, 0, 0))]
    return pl.pallas_call(
        functools.partial(_mix_prompt_body, tm=tm, tiles_per_seq=tiles_per_seq, parts=2),
        out_shape=[jax.ShapeDtypeStruct((n, D_MODEL), F32),
                   jax.ShapeDtypeStruct((n_seq, CONV_W - 1, D_A), F32)],
        grid=(n // tm,), in_specs=in_specs, out_specs=out_specs,
        scratch_shapes=[pltpu.VMEM((tm + SUBLANES, D_A), F32)],
        compiler_params=_params(1), name="mix_prompt")(
            x, g.reshape(1, D_MODEL), win, cw, lng.reshape(1, D_B), lnb.reshape(1, D_B),
            sgw, sgb_tbl, wout)


def _mix_decode_body(x_ref, g_ref, win_ref, cw_ref, past_ref, lng_ref, lnb_ref, w00_ref, b0_ref,
                     wout_ref, o_ref, st_ref, vn_ref):
    x = x_ref[...]
    h = _rms(x, g_ref[...], EPS).astype(BF16)
    proj = jnp.dot(h, win_ref[...], preferred_element_type=F32)
    bg = proj[:, 0:D_A]
    cg = proj[:, D_A:2 * D_A]
    hx = proj[:, 2 * D_A:3 * D_A]
    u = proj[:, 3 * D_A:3 * D_A + D_B]
    v = proj[:, 3 * D_A + D_B:]
    xc = cg * hx
    p0 = past_ref[:, 0:D_A]
    p1 = past_ref[:, D_A:2 * D_A]
    cw = cw_ref[...]
    ya = bg * (cw[0:1, :] * p0 + cw[1:2, :] * p1 + cw[2:3, :] * xc)
    st_ref[:, 0:D_A] = p1
    st_ref[:, D_A:2 * D_A] = xc
    vn = _layer_norm(v, lng_ref[...], lnb_ref[...])
    vn_ref[...] = vn
    yb = u * (w00_ref[...] * vn + b0_ref[...])
    mixin = jnp.concatenate([ya, yb], axis=1).astype(BF16)
    o_ref[...] = x + jnp.dot(mixin, wout_ref[...], preferred_element_type=F32)


def _mix_decode(x, g, win, cw, past, lng, lnb, w00, b0, wout):
    n = x.shape[0]
    d_in = 3 * D_A + 2 * D_B
    full = lambda shape: pl.BlockSpec(shape, lambda i: (0,) * len(shape))
    return pl.pallas_call(
        _mix_decode_body,
        out_shape=[jax.ShapeDtypeStruct((n, D_MODEL), F32),
                   jax.ShapeDtypeStruct((n, (CONV_W - 1) * D_A), F32),
                   jax.ShapeDtypeStruct((n, D_B), F32)],
        grid=(1,),
        in_specs=[full((n, D_MODEL)), full((1, D_MODEL)), full((D_MODEL, d_in)),
                  full((CONV_W, D_A)), full((n, (CONV_W - 1) * D_A)), full((1, D_B)),
                  full((1, D_B)), full((1, D_B)), full((1, D_B)), full((D_A + D_B, D_MODEL))],
        out_specs=[full((n, D_MODEL)), full((n, (CONV_W - 1) * D_A)), full((n, D_B))],
        compiler_params=_params(1), name="mix_decode")(
            x, g.reshape(1, D_MODEL), win, cw, past, lng.reshape(1, D_B), lnb.reshape(1, D_B),
            w00, b0, wout)


def _rope_angles(pos):
    inv = ROPE_THETA ** (-(jnp.arange(ROT_HALF, dtype=F32) * 2.0) / ROT_DIM)
    ang = pos.astype(F32)[:, None] * inv[None, :]
    return jnp.cos(ang), jnp.sin(ang)


def _rope_tables(pos):
    cos, sin = _rope_angles(pos)
    pad = jnp.zeros((pos.shape[0], HEAD_DIM - ROT_DIM), F32)
    cos64 = jnp.concatenate([cos, cos, pad + 1.0], axis=1)
    sa64 = jnp.concatenate([jnp.zeros_like(sin), sin, pad], axis=1)
    sb64 = jnp.concatenate([-sin, jnp.zeros_like(sin), pad], axis=1)
    two = lambda a: jnp.concatenate([a, a], axis=1)
    return two(cos64), two(sa64), two(sb64)


def _rope(t, cos, sa, sb):
    blocks = []
    for b in range(t.shape[1] // LANES):
        blk = t[:, b * LANES:(b + 1) * LANES]
        blocks.append(blk * cos + pltpu.roll(blk, ROT_HALF, 1) * sa
                      + pltpu.roll(blk, LANES - ROT_HALF, 1) * sb)
    return jnp.concatenate(blocks, axis=1)


def _qkv_prompt_body(x_ref, g_ref, wq_ref, wkt_ref, wv_ref, cos_ref, sa_ref, sb_ref, cost_ref,
                     sint_ref, q_ref, kt_ref, ktb_ref, v_ref, vb_ref, *, tm):
    h = _rms(x_ref[...], g_ref[...], EPS).astype(BF16)
    q = jnp.dot(h, wq_ref[...], preferred_element_type=F32)
    v = jnp.dot(h, wv_ref[...], preferred_element_type=F32)
    q = _rope(q, cos_ref[...], sa_ref[...], sb_ref[...]) * (HEAD_DIM ** -0.5 * LOG2_E)
    q_ref[...] = q.astype(BF16)

    kt = lax.dot_general(wkt_ref[...], h, (((1,), (1,)), ((), ())), preferred_element_type=F32)

    vb_ref[...] = v.astype(BF16)
    for hd in range(N_HEADS):
        v_ref[pl.ds(hd, tm, stride=N_HEADS), :] = v[:, hd * V_DIM:(hd + 1) * V_DIM]

    c, s = cost_ref[...], sint_ref[...]
    pieces = []
    for sh in range(N_SUB):
        base = sh * HEAD_DIM
        x1 = kt[base:base + ROT_HALF, :]
        x2 = kt[base + ROT_HALF:base + ROT_DIM, :]
        pieces += [x1 * c - x2 * s, x2 * c + x1 * s, kt[base + ROT_DIM:base + HEAD_DIM, :]]
    kt = jnp.concatenate(pieces, axis=0)
    kt_ref[0] = kt
    ktb_ref[0, 0] = kt.astype(BF16)


def _qkv_prompt(x, g, wq, wkt, wv, tables, tables_t, *, n_seq, seq, tm):
    n = x.shape[0]
    tps = seq // tm
    row = pl.BlockSpec((tm, D_ATT), lambda i: (i, 0))
    tbl = pl.BlockSpec((tm, LANES), lambda i: (i % tps, 0))
    tbl_t = pl.BlockSpec((ROT_HALF, tm), lambda i: (0, i % tps))
    w_spec = _const_spec((D_MODEL, D_ATT))
    return pl.pallas_call(
        functools.partial(_qkv_prompt_body, tm=tm),
        out_shape=[jax.ShapeDtypeStruct((n, D_ATT), BF16),
                   jax.ShapeDtypeStruct((n_seq, D_ATT, seq), F32),
                   jax.ShapeDtypeStruct((n_seq, tps, D_ATT, tm), BF16),
                   jax.ShapeDtypeStruct((n * N_HEADS, V_DIM), F32),
                   jax.ShapeDtypeStruct((n, D_ATT), BF16)],
        grid=(n // tm,),
        in_specs=[pl.BlockSpec((tm, D_MODEL), lambda i: (i, 0)), _const_spec((1, D_MODEL)),
                  w_spec, w_spec, w_spec, tbl, tbl, tbl, tbl_t, tbl_t],
        out_specs=[row,
                   pl.BlockSpec((1, D_ATT, tm), lambda i: (i // tps, 0, i % tps)),
                   pl.BlockSpec((1, 1, D_ATT, tm), lambda i: (i // tps, i % tps, 0, 0)),
                   pl.BlockSpec((tm * N_HEADS, V_DIM), lambda i: (i, 0)),
                   row],
        compiler_params=_params(1), name="qkv_prompt")(
            x, g.reshape(1, D_MODEL), wq, wkt, wv, *tables, *tables_t)


def _qkv_decode_body(x_ref, g_ref, w_ref, cos_ref, sa_ref, sb_ref, q_ref, k_ref, v_ref):
    h = _rms(x_ref[...], g_ref[...], EPS).astype(BF16)
    qkv = jnp.dot(h, w_ref[...], preferred_element_type=F32)
    cos, sa, sb = cos_ref[...], sa_ref[...], sb_ref[...]
    q_ref[...] = _rope(qkv[:, 0:D_ATT], cos, sa, sb) * (HEAD_DIM ** -0.5)
    k_ref[...] = _rope(qkv[:, D_ATT:2 * D_ATT], cos, sa, sb)
    v_ref[...] = qkv[:, 2 * D_ATT:]


def _qkv_decode(x, g, w, tables):
    n = x.shape[0]
    full = lambda shape: pl.BlockSpec(shape, lambda i: (0,) * len(shape))
    out = jax.ShapeDtypeStruct((n, D_ATT), F32)
    return pl.pallas_call(
        _qkv_decode_body, out_shape=[out, out, out], grid=(1,),
        in_specs=[full((n, D_MODEL)), full((1, D_MODEL)), full((D_MODEL, 3 * D_ATT)),
                  full((n, LANES)), full((n, LANES)), full((n, LANES))],
        out_specs=[full((n, D_ATT))] * 3,
        compiler_params=_params(1), name="qkv_decode")(x, g.reshape(1, D_MODEL), w, *tables)


def _lambda(lq, lam_init):
    a = jnp.sum(lq[0:1, :] * lq[1:2, :], axis=-1, keepdims=True)
    b = jnp.sum(lq[2:3, :] * lq[3:4, :], axis=-1, keepdims=True)
    return jnp.exp(a) - jnp.exp(b) + lam_init


def _attn_body(pt_ref, lq_ref, sg_ref, q_ref, kt_ref, v_ref, qd_ref, kn_ref, vn_ref, *refs,
               tq, rb_full, rb_diag, ck, page, pages, dec_steps, lam_init):
    del pt_ref
    k_refs = refs[:pages]
    v_refs = refs[pages:2 * pages]
    (o_ref, od_ref, q2_scr, m_scr, l_scr, acc_scr,
     qx_scr, sd_scr, pd_scr, md_scr, ld_scr, al_scr, accd_scr) = refs[2 * pages:]
    qi = pl.program_id(2)
    step_id = (pl.program_id(0) * pl.num_programs(1) + pl.program_id(1)) * pl.num_programs(2) + qi
    jd = lax.rem(step_id, dec_steps)
    sub = lax.broadcasted_iota(jnp.int32, (N_SUB, D_ATT), 0)
    lane_d = lax.broadcasted_iota(jnp.int32, (N_SUB, D_ATT), 1)

    def q_expanded():
        return jnp.where(lax.shift_right_logical(lane_d, 6) == sub, qd_ref[0], 0.0)

    @pl.when(jd == 0)
    def _():
        qx_scr[...] = q_expanded().astype(BF16)
        md_scr[...] = jnp.full((N_SUB, 1), NEG_INF, F32)
        ld_scr[...] = jnp.zeros((N_SUB, 1), F32)
        accd_scr[...] = jnp.zeros((N_SUB, D_ATT), F32)

    def decode_scores(r):
        sd_scr[:, r * page:(r + 1) * page] = jnp.dot(
            qx_scr[...], k_refs[r][...].astype(BF16), preferred_element_type=F32)

    def decode_softmax():
        s = sd_scr[...]
        m_prev = md_scr[...]
        m_new = jnp.maximum(m_prev, jnp.max(s, axis=-1, keepdims=True))
        alpha = jnp.exp(m_prev - m_new)
        p = jnp.exp(s - m_new)
        ld_scr[...] = alpha * ld_scr[...] + jnp.sum(p, axis=-1, keepdims=True)
        pd_scr[...] = p.astype(BF16)
        al_scr[...] = alpha
        md_scr[...] = m_new

    pv_sum = []

    def decode_values(r):
        v_page = jnp.concatenate(
            [v_refs[r][pl.ds(hd, page, stride=N_HEADS), :] for hd in range(N_HEADS)], axis=1)
        d = jnp.dot(pd_scr[:, r * page:(r + 1) * page], v_page.astype(BF16),
                    preferred_element_type=F32)
        pv_sum[:] = [d if not pv_sum else pv_sum[0] + d]

    def decode_accumulate():
        accd_scr[...] = al_scr[...] * accd_scr[...] + pv_sum[0]

    n_rb_diag = 2 * tq // rb_diag
    half = n_rb_diag // 2
    per_block = pages // half
    pieces = [[] for _ in range(n_rb_diag)]
    for r in range(pages):
        pieces[r // per_block].append(functools.partial(decode_scores, r))
        pieces[half + r // per_block].append(functools.partial(decode_values, r))
    pieces[half - 1].append(decode_softmax)
    pieces[n_rb_diag - 1].append(decode_accumulate)

    q = q_ref[0]
    lane = lax.broadcasted_iota(jnp.int32, (tq, V_DIM), 1)
    zero = jnp.zeros_like(q)
    q2_scr[0:tq, :] = jnp.where(lane < HEAD_DIM, q, zero)
    q2_scr[tq:2 * tq, :] = jnp.where(lane >= HEAD_DIM, q, zero)
    m_scr[...] = jnp.full((2 * tq, LANES), NEG_INF, F32)
    l_scr[...] = jnp.zeros((2 * tq, LANES), F32)
    acc_scr[...] = jnp.zeros((2 * tq, V_DIM), F32)

    def step(j, diagonal):
        start = pl.multiple_of(j * tq, tq)
        rb = rb_diag if diagonal else rb_full
        n_rb = 2 * tq // rb

        def n_keys(r):
            return (r * rb) % tq + rb if diagonal else tq

        def scores(r):
            n_k = n_keys(r)
            cols = [kt_ref[0, j * (tq // ck) + c // ck, :, 0:min(ck, n_k - c)]
                    for c in range(0, n_k, ck)]
            kt = cols[0] if len(cols) == 1 else jnp.concatenate(cols, axis=1)
            return jnp.dot(q2_scr[r * rb:(r + 1) * rb, :], kt,
                           preferred_element_type=F32)

        pending = [scores(r) for r in range(min(QK_AHEAD, n_rb))]
        for r in range(n_rb):
            rows = slice(r * rb, (r + 1) * rb)
            first = (r * rb) % tq
            n_k = n_keys(r)
            vc = v_ref[0, pl.ds(start, n_k), :]
            s = pending.pop(0)
            if r + QK_AHEAD < n_rb:
                pending.append(scores(r + QK_AHEAD))
            if diagonal:
                r_i = lax.broadcasted_iota(jnp.int32, (rb, rb), 0)
                c_i = lax.broadcasted_iota(jnp.int32, (rb, rb), 1)
                tri = jnp.where(c_i <= r_i, s[:, first:], NEG_INF)
                s = tri if first == 0 else jnp.concatenate([s[:, :first], tri], axis=1)
            m_prev = m_scr[rows, :]
            m_new = jnp.maximum(m_prev, jnp.max(s, axis=-1, keepdims=True))
            alpha = jnp.exp2(m_prev - m_new)
            p = jnp.exp2(s - jnp.tile(m_new, (1, n_k // LANES)))
            l_scr[rows, :] = alpha * l_scr[rows, :] + jnp.sum(p, axis=-1, keepdims=True)
            acc_scr[rows, :] = alpha * acc_scr[rows, :] + jnp.dot(
                p.astype(BF16), vc, preferred_element_type=F32)
            m_scr[rows, :] = m_new
            if diagonal:
                for piece in pieces[r]:
                    piece()

    def loop_body(j, carry):
        step(j, False)
        return carry

    lax.fori_loop(0, qi, loop_body, 0)
    step(qi, True)

    o2 = acc_scr[...] / l_scr[...]
    lam = _lambda(lq_ref[...], lam_init)
    o = o2[0:tq, :] - lam * o2[tq:2 * tq, :]
    o = _rms(o, sg_ref[...], SUBLN_EPS) * (1.0 - lam_init)
    o_ref[0] = o.astype(o_ref.dtype)

    @pl.when(jd == dec_steps - 1)
    def _():
        s_new = jnp.sum(q_expanded() * kn_ref[0], axis=-1, keepdims=True)
        m_prev = md_scr[...]
        m_fin = jnp.maximum(m_prev, s_new)
        alpha = jnp.exp(m_prev - m_fin)
        p_new = jnp.exp(s_new - m_fin)
        l_fin = alpha * ld_scr[...] + p_new
        a = (alpha * accd_scr[...] + p_new * vn_ref[0]) / l_fin
        w = jnp.where((sub & 1) == 0, 1.0, -lam)
        own = lax.shift_right_logical(lane_d, 7) == lax.shift_right_logical(sub, 1)
        od_ref[0] = jnp.sum(jnp.where(own, a * w, 0.0), axis=0, keepdims=True)


def _attention(q, ktb, vb, qd, kd, vd, cache_k, cache_v, page_table, lq, sg, *, n_seq, seq, tq,
               rb_full, rb_diag, lam_init):
    ck = ktb.shape[3]
    n_dec = qd.shape[0]
    n_phys, page = cache_k.shape[0], cache_k.shape[1]
    n_pages = page_table.shape[1]
    n_q = seq // tq
    n_steps = n_seq * N_HEADS * n_q
    dec_steps = n_steps // n_dec
    pages = n_pages // dec_steps
    assert dec_steps * n_dec == n_steps and pages * dec_steps == n_pages
    assert pages % (tq // rb_diag) == 0
    q3 = q.reshape(n_seq, seq, D_ATT)
    v3 = vb.reshape(n_seq, seq, D_ATT)
    c_k = jnp.transpose(cache_k, (0, 2, 3, 1)).reshape(n_phys, D_ATT, page)
    c_v = cache_v.reshape(n_phys, page * N_HEADS, V_DIM)

    def lin(b, h, i):
        return (b * N_HEADS + h) * n_q + i

    q_spec = pl.BlockSpec((1, tq, V_DIM), lambda b, h, i, pt: (b, i, h))
    d_spec = pl.BlockSpec((1, 1, D_ATT), lambda b, h, i, pt: (lin(b, h, i) // dec_steps, 0, 0))

    def page_spec(shape, r):
        def index(b, h, i, pt):
            t = lin(b, h, i)
            return (pt[t // dec_steps, (t % dec_steps) * pages + r], 0, 0)
        return pl.BlockSpec((None,) + shape, index)

    grid_spec = pltpu.PrefetchScalarGridSpec(
        num_scalar_prefetch=1, grid=(n_seq, N_HEADS, n_q),
        in_specs=[pl.BlockSpec((4, HEAD_DIM), lambda b, h, i, pt: (0, 0)),
                  pl.BlockSpec((1, V_DIM), lambda b, h, i, pt: (0, 0)), q_spec,
                  pl.BlockSpec((1, seq // ck, V_DIM, ck), lambda b, h, i, pt: (b, 0, h, 0)),
                  pl.BlockSpec((1, seq, V_DIM), lambda b, h, i, pt: (b, 0, h)),
                  d_spec, d_spec, d_spec]
        + [page_spec((D_ATT, page), r) for r in range(pages)]
        + [page_spec((page * N_HEADS, V_DIM), r) for r in range(pages)],
        out_specs=[q_spec, d_spec],
        scratch_shapes=[pltpu.VMEM((2 * tq, V_DIM), BF16), pltpu.VMEM((2 * tq, LANES), F32),
                        pltpu.VMEM((2 * tq, LANES), F32), pltpu.VMEM((2 * tq, V_DIM), F32),
                        pltpu.VMEM((N_SUB, D_ATT), BF16), pltpu.VMEM((N_SUB, pages * page), F32),
                        pltpu.VMEM((N_SUB, pages * page), BF16), pltpu.VMEM((N_SUB, 1), F32),
                        pltpu.VMEM((N_SUB, 1), F32), pltpu.VMEM((N_SUB, 1), F32),
                        pltpu.VMEM((N_SUB, D_ATT), F32)])
    out, out_d = pl.pallas_call(
        functools.partial(_attn_body, tq=tq, rb_full=rb_full, rb_diag=rb_diag, ck=ck, page=page,
                          pages=pages, dec_steps=dec_steps, lam_init=lam_init),
        out_shape=[jax.ShapeDtypeStruct((n_seq, seq, D_ATT), BF16),
                   jax.ShapeDtypeStruct((n_dec, 1, D_ATT), F32)],
        grid_spec=grid_spec, compiler_params=_params(3), name="attention")(
            page_table, lq, sg.reshape(1, V_DIM), q3, ktb, v3, qd.reshape(n_dec, 1, D_ATT),
            kd.reshape(n_dec, 1, D_ATT), vd.reshape(n_dec, 1, D_ATT),
            *([c_k] * pages), *([c_v] * pages))
    return out.reshape(n_seq * seq, D_ATT), out_d.reshape(n_dec, D_ATT)


def _attn_out_decode_body(x_ref, a_ref, w_ref, sg_ref, out_ref, *, lam_init):
    a = a_ref[...]
    sg = sg_ref[...]
    blocks = []
    for hd in range(N_HEADS):
        blk = a[:, hd * V_DIM:(hd + 1) * V_DIM]
        blocks.append(_rms(blk, sg, SUBLN_EPS) * (1.0 - lam_init))
    a = jnp.concatenate(blocks, axis=1).astype(BF16)
    out_ref[...] = x_ref[...] + jnp.dot(a, w_ref[...], preferred_element_type=F32)


def _attn_out_decode(x, a, w, sg, *, lam_init):
    n = x.shape[0]
    full = lambda shape: pl.BlockSpec(shape, lambda i: (0,) * len(shape))
    return pl.pallas_call(
        functools.partial(_attn_out_decode_body, lam_init=lam_init),
        out_shape=jax.ShapeDtypeStruct((n, D_MODEL), F32), grid=(1,),
        in_specs=[full((n, D_MODEL)), full((n, D_ATT)), full((D_ATT, D_MODEL)), full((1, V_DIM))],
        out_specs=full((n, D_MODEL)),
        compiler_params=_params(1), name="attn_out_decode")(x, a, w, sg.reshape(1, V_DIM))


def kernel(x_prompt, x_sample, cache_conv, cache_k, cache_v, page_table, norm_g, ffn_w_gate,
           ffn_w_up, ffn_w_down, w_in_mix, conv_w, sg_ln_g, sg_ln_b, sg_w, sg_b, w_out_mix, w_qkv,
           lambda_qk, subln_g, w_o, final_norm_g):
    n_seq, seq, _ = x_prompt.shape
    n_dec, dec_seq, _ = x_sample.shape
    tm = 512
    assert dec_seq == 1 and seq % tm == 0
    depth = norm_g.shape[0]
    past_len = page_table.shape[1] * cache_k.shape[2]
    assert past_len % CHUNK == 0

    wg, wu, wd = ffn_w_gate.astype(BF16), ffn_w_up.astype(BF16), ffn_w_down.astype(BF16)
    win, wout = w_in_mix.astype(BF16), w_out_mix.astype(BF16)
    wqkv, wo = w_qkv.astype(BF16), w_o.astype(BF16)

    xp = x_prompt.reshape(n_seq * seq, D_MODEL)
    xs = x_sample.reshape(n_dec, D_MODEL)
    ffn_p = functools.partial(_ffn, tm=tm, ff_chunk=256)
    ffn_s = functools.partial(_ffn, tm=n_dec, ff_chunk=D_FF)
    pos_p = jnp.arange(seq)
    pos_s = jnp.broadcast_to(past_len + jnp.arange(dec_seq), (n_dec,))
    tables_p, tables_s = _rope_tables(pos_p), _rope_tables(pos_s)
    tables_pt = tuple(a.T for a in _rope_angles(pos_p))

    conv_p, conv_s, chunk_s, k_p, v_p, k_s, v_s = [], [], [], [], [], [], []
    attn_p = None
    for l in range(depth):
        last = l == depth - 1
        xp = ffn_p(xp, norm_g[l, 0], wg, wu, wd, (l, 0))
        xs = ffn_s(xs, norm_g[l, 0], wg, wu, wd, (l, 0))
        if l % 2 == 0:
            i = l // 2
            sgb_tbl = jnp.broadcast_to(sg_b[i][:, :, None], (G_B, CHUNK, LANES))
            xp, st = _mix_prompt(xp, norm_g[l, 1], win[i], conv_w[i], sg_ln_g[i], sg_ln_b[i],
                                 sg_w[i], sgb_tbl, wout[i], n_seq=n_seq, seq=seq, tm=tm)
            conv_p.append(st)
            w00 = jnp.repeat(sg_w[i][:, 0, 0], D_B // G_B).reshape(1, D_B)
            b0 = jnp.repeat(sg_b[i][:, 0], D_B // G_B).reshape(1, D_B)
            xs, st, vn = _mix_decode(xs, norm_g[l, 1], win[i], conv_w[i],
                                     cache_conv[i].reshape(n_dec, (CONV_W - 1) * D_A),
                                     sg_ln_g[i], sg_ln_b[i], w00, b0, wout[i])
            conv_s.append(st.reshape(n_dec, CONV_W - 1, D_A))
            chunk_s.append(vn.reshape(n_dec, dec_seq, D_B))
        else:
            j = l // 2
            lam_init = 0.8 - 0.6 * math.exp(-0.3 * l)
            wq = wqkv[j][:, 0:D_ATT]
            wkt = wqkv[j][:, D_ATT:2 * D_ATT].T
            wv = wqkv[j][:, 2 * D_ATT:]
            q, kt, ktb, v, vb = _qkv_prompt(xp, norm_g[l, 1], wq, wkt, wv, tables_p, tables_pt,
                                            n_seq=n_seq, seq=seq, tm=tm)
            qd, kd, vd = _qkv_decode(xs, norm_g[l, 1], wqkv[j], tables_s)
            a, ad = _attention(q, ktb, vb, qd, kd, vd, cache_k[j], cache_v[j], page_table,
                               lambda_qk[j], subln_g[j], n_seq=n_seq, seq=seq, tq=1024,
                               rb_full=512, rb_diag=256, lam_init=lam_init)
            attn_p = (a, wo[j])
            xs = _attn_out_decode(xs, ad, wo[j], subln_g[j], lam_init=lam_init)
            k_p.append(jnp.transpose(kt.reshape(n_seq, N_SUB, HEAD_DIM, seq), (0, 3, 1, 2)))
            v_p.append(v.reshape(n_seq, seq, N_HEADS, V_DIM))
            k_s.append(kd.reshape(n_dec, dec_seq, N_SUB, HEAD_DIM))
            v_s.append(vd.reshape(n_dec, dec_seq, N_HEADS, V_DIM))
        fg = final_norm_g if last else None
        if attn_p is None:
            xp = ffn_p(xp, norm_g[l, 2], wg, wu, wd, (l, 1), final_g=fg)
        else:
            xp = ffn_p(xp, norm_g[l, 2], wg, wu, wd, (l, 1), attn=attn_p[0], w_o=attn_p[1],
                       final_g=fg)
            attn_p = None
        xs = ffn_s(xs, norm_g[l, 2], wg, wu, wd, (l, 1), final_g=fg)

    return (xp.reshape(n_seq, seq, D_MODEL), xs.reshape(n_dec, dec_seq, D_MODEL),
            jnp.stack(conv_p), jnp.stack(conv_s), jnp.stack(chunk_s),
            jnp.stack(k_p), jnp.stack(v_p), jnp.stack(k_s), jnp.stack(v_s))
```

```python
import functools
import math

import jax
import jax.numpy as jnp
from jax import lax
from jax.experimental import pallas as pl
from jax.experimental.pallas import tpu as pltpu

F32 = jnp.float32
BF16 = jnp.bfloat16

D_MODEL = 1024
D_A = D_MODEL // 2
D_B = D_MODEL // 2
CONV_W = 3
CHUNK = 128
G_B = 4
N_HEADS = 8
HEAD_DIM = 64
N_SUB = 2 * N_HEADS
V_DIM = 2 * HEAD_DIM
D_ATT = N_SUB * HEAD_DIM
ROT_DIM = HEAD_DIM // 4
ROT_HALF = ROT_DIM // 2
ROPE_THETA = 500000.0
D_FF = 2816
EPS = 1e-6
SUBLN_EPS = 1e-5
LN_EPS = 1e-5
NEG_INF = -1e30
LOG2_E = math.log2(math.e)

LANES = 128
SUBLANES = 8
VMEM_LIMIT = 56 * 1024 * 1024
MIX_SLAB = 256
QK_AHEAD = 1


def _rms(x, g, eps):
    ms = jnp.mean(x * x, axis=-1, keepdims=True)
    return x * lax.rsqrt(ms + eps) * g


def _const_spec(shape):
    return pl.BlockSpec(shape, lambda *_: (0,) * len(shape), pipeline_mode=pl.Buffered(1))


def _stacked_spec(shape, index):
    return pl.BlockSpec((None,) * len(index) + shape,
                        lambda *_: tuple(index) + (0,) * len(shape),
                        pipeline_mode=pl.Buffered(1))


def _params(n_axes):
    return pltpu.CompilerParams(dimension_semantics=("arbitrary",) * n_axes,
                                vmem_limit_bytes=VMEM_LIMIT)


def _ffn_body(x_ref, xs_ref, g_ref, wg_ref, wu_ref, wd_ref, *rest, ff_chunk, has_attn, has_final):
    rest = list(rest)
    os_ref = rest.pop()
    o_ref = rest.pop()
    final_g = rest.pop()[...] if has_final else None

    def ffn_rows(x):
        h = _rms(x, g_ref[...], EPS).astype(BF16)
        acc = None
        for c in range(D_FF // ff_chunk):
            sl = slice(c * ff_chunk, (c + 1) * ff_chunk)
            gate = jnp.dot(h, wg_ref[:, sl], preferred_element_type=F32)
            up = jnp.dot(h, wu_ref[:, sl], preferred_element_type=F32)
            a = (gate * jax.nn.sigmoid(gate) * up).astype(BF16)
            d = jnp.dot(a, wd_ref[sl, :], preferred_element_type=F32)
            acc = d if acc is None else acc + d
        y = x + 0.5 * acc
        return _rms(y, final_g, EPS) if has_final else y

    x = x_ref[...]
    if has_attn:
        a_ref, wo_ref = rest
        x = x + jnp.dot(a_ref[...], wo_ref[...], preferred_element_type=F32)
    o_ref[...] = ffn_rows(x)

    @pl.when(pl.program_id(0) == pl.num_programs(0) - 1)
    def _():
        os_ref[...] = ffn_rows(xs_ref[...])


def _ffn(x, xs, g, wg, wu, wd, idx, *, tm, ff_chunk, attn=None, w_o=None, final_g=None):
    n, ns = x.shape[0], xs.shape[0]
    row = pl.BlockSpec((tm, D_MODEL), lambda i: (i, 0))
    side = pl.BlockSpec((ns, D_MODEL), lambda i: (0, 0))
    in_specs = [row, side, _const_spec((1, D_MODEL)), _stacked_spec((D_MODEL, D_FF), idx),
                _stacked_spec((D_MODEL, D_FF), idx), _stacked_spec((D_FF, D_MODEL), idx)]
    args = [x, xs, g.reshape(1, D_MODEL), wg, wu, wd]
    if attn is not None:
        in_specs += [pl.BlockSpec((tm, D_ATT), lambda i: (i, 0)), _const_spec((D_ATT, D_MODEL))]
        args += [attn, w_o]
    if final_g is not None:
        in_specs.append(_const_spec((1, D_MODEL)))
        args.append(final_g.reshape(1, D_MODEL))
    return pl.pallas_call(
        functools.partial(_ffn_body, ff_chunk=ff_chunk, has_attn=attn is not None,
                          has_final=final_g is not None),
        out_shape=[jax.ShapeDtypeStruct((n, D_MODEL), F32),
                   jax.ShapeDtypeStruct((ns, D_MODEL), F32)],
        grid=(n // tm,), in_specs=in_specs, out_specs=[row, side],
        compiler_params=_params(1), name="ffn")(*args)


def _layer_norm(v, g, b):
    mu = jnp.mean(v, axis=-1, keepdims=True)
    vc = v - mu
    var = jnp.mean(vc * vc, axis=-1, keepdims=True)
    return vc * lax.rsqrt(var + LN_EPS) * g + b


def _mix_prompt_body(x_ref, g_ref, win_ref, cw_ref, lng_ref, lnb_ref, sgw_ref, sgb_ref, wout_ref,
                     o_ref, st_ref, xc_scr, *, tm, tiles_per_seq, parts):
    i = pl.program_id(0)
    pm = tm // parts
    n_chunks = pm // CHUNK
    r_i = lax.broadcasted_iota(jnp.int32, (CHUNK, CHUNK), 0)
    c_i = lax.broadcasted_iota(jnp.int32, (CHUNK, CHUNK), 1)
    w_tril = [jnp.where(c_i <= r_i, sgw_ref[gi], 0.0).astype(BF16) for gi in range(G_B)]
    cw = cw_ref[...]

    def in_proj(k):
        h = _rms(x_ref[k * pm:(k + 1) * pm, :], g_ref[...], EPS).astype(BF16)
        return jnp.dot(h, win_ref[...], preferred_element_type=F32)

    @pl.when(i % tiles_per_seq == 0)
    def _():
        xc_scr[0:SUBLANES, :] = jnp.zeros((SUBLANES, D_A), F32)

    proj_next = in_proj(0)
    for k in range(parts):
        proj = proj_next
        if k + 1 < parts:
            proj_next = in_proj(k + 1)
        bg = proj[:, 0:D_A]
        cg = proj[:, D_A:2 * D_A]
        hx = proj[:, 2 * D_A:3 * D_A]
        u = proj[:, 3 * D_A:3 * D_A + D_B]
        v = proj[:, 3 * D_A + D_B:]

        xc = cg * hx
        base = SUBLANES + k * pm
        xc_scr[base:base + pm, :] = xc
        xm1 = xc_scr[base - 1:base - 1 + pm, :]
        xm2 = xc_scr[base - 2:base - 2 + pm, :]
        ya = bg * (cw[0:1, :] * xm2 + cw[1:2, :] * xm1 + cw[2:3, :] * xc)
        if k == parts - 1:
            st_ref[0] = xc[pm - (CONV_W - 1):pm, :]

        vn = _layer_norm(v, lng_ref[...], lnb_ref[...]).astype(BF16)
        zs = []
        for gi in range(G_B):
            vg = vn[:, gi * LANES:(gi + 1) * LANES]
            wide = jnp.concatenate([vg[c * CHUNK:(c + 1) * CHUNK, :] for c in range(n_chunks)],
                                   axis=1)
            zw = jnp.dot(w_tril[gi], wide, preferred_element_type=F32)
            bias = sgb_ref[gi]
            zs.append(jnp.concatenate(
                [zw[:, c * CHUNK:(c + 1) * CHUNK] + bias for c in range(n_chunks)], axis=0))
        yb = u * jnp.concatenate(zs, axis=1)

        mixin = jnp.concatenate([ya, yb], axis=1).astype(BF16)
        o_ref[k * pm:(k + 1) * pm, :] = x_ref[k * pm:(k + 1) * pm, :] + jnp.dot(
            mixin, wout_ref[...], preferred_element_type=F32)

    xc_scr[0:SUBLANES, :] = xc_scr[tm:tm + SUBLANES, :]


def _mix_prompt(x, g, win, cw, lng, lnb, sgw, sgb_tbl, wout, *, n_seq, seq, tm):
    n = x.shape[0]
    tiles_per_seq = seq // tm
    d_in = 3 * D_A + 2 * D_B
    row = pl.BlockSpec((tm, D_MODEL), lambda i: (i, 0))
    in_specs = [row, _const_spec((1, D_MODEL)), _const_spec((D_MODEL, d_in)),
                _const_spec((CONV_W, D_A)), _const_spec((1, D_B)), _const_spec((1, D_B)),
                _const_spec((G_B, CHUNK, CHUNK)), _const_spec((G_B, CHUNK, LANES)),
                _const_spec((D_A + D_B, D_MODEL))]
    out_specs = [row, pl.BlockSpec((1, CONV_W - 1, D_A), lambda i: (i // tiles_per_seq, 0, 0))]
    return pl.pallas_call(
        functools.partial(_mix_prompt_body, tm=tm, tiles_per_seq=tiles_per_seq,
                          parts=tm // MIX_SLAB),
        out_shape=[jax.ShapeDtypeStruct((n, D_MODEL), F32),
                   jax.ShapeDtypeStruct((n_seq, CONV_W - 1, D_A), F32)],
        grid=(n // tm,), in_specs=in_specs, out_specs=out_specs,
        scratch_shapes=[pltpu.VMEM((tm + SUBLANES, D_A), F32)],
        compiler_params=_params(1), name="mix_prompt")(
            x, g.reshape(1, D_MODEL), win, cw, lng.reshape(1, D_B), lnb.reshape(1, D_B),
            sgw, sgb_tbl, wout)


def _mix_decode_body(x_ref, g_ref, win_ref, cw_ref, past_ref, lng_ref, lnb_ref, w00_ref, b0_ref,
                     wout_ref, o_ref, st_ref, vn_ref):
    x = x_ref[...]
    h = _rms(x, g_ref[...], EPS).astype(BF16)
    proj = jnp.dot(h, win_ref[...], preferred_element_type=F32)
    bg = proj[:, 0:D_A]
    cg = proj[:, D_A:2 * D_A]
    hx = proj[:, 2 * D_A:3 * D_A]
    u = proj[:, 3 * D_A:3 * D_A + D_B]
    v = proj[:, 3 * D_A + D_B:]
    xc = cg * hx
    p0 = past_ref[:, 0:D_A]
    p1 = past_ref[:, D_A:2 * D_A]
    cw = cw_ref[...]
    ya = bg * (cw[0:1, :] * p0 + cw[1:2, :] * p1 + cw[2:3, :] * xc)
    st_ref[:, 0:D_A] = p1
    st_ref[:, D_A:2 * D_A] = xc
    vn = _layer_norm(v, lng_ref[...], lnb_ref[...])
    vn_ref[...] = vn
    yb = u * (w00_ref[...] * vn + b0_ref[...])
    mixin = jnp.concatenate([ya, yb], axis=1).astype(BF16)
    o_ref[...] = x + jnp.dot(mixin, wout_ref[...], preferred_element_type=F32)


def _mix_decode(x, g, win, cw, past, lng, lnb, w00, b0, wout):
    n = x.shape[0]
    d_in = 3 * D_A + 2 * D_B
    full = lambda shape: pl.BlockSpec(shape, lambda i: (0,) * len(shape))
    return pl.pallas_call(
        _mix_decode_body,
        out_shape=[jax.ShapeDtypeStruct((n, D_MODEL), F32),
                   jax.ShapeDtypeStruct((n, (CONV_W - 1) * D_A), F32),
                   jax.ShapeDtypeStruct((n, D_B), F32)],
        grid=(1,),
        in_specs=[full((n, D_MODEL)), full((1, D_MODEL)), full((D_MODEL, d_in)),
                  full((CONV_W, D_A)), full((n, (CONV_W - 1) * D_A)), full((1, D_B)),
                  full((1, D_B)), full((1, D_B)), full((1, D_B)), full((D_A + D_B, D_MODEL))],
        out_specs=[full((n, D_MODEL)), full((n, (CONV_W - 1) * D_A)), full((n, D_B))],
        compiler_params=_params(1), name="mix_decode")(
            x, g.reshape(1, D_MODEL), win, cw, past, lng.reshape(1, D_B), lnb.reshape(1, D_B),
            w00, b0, wout)


def _rope_angles(pos):
    inv = ROPE_THETA ** (-(jnp.arange(ROT_HALF, dtype=F32) * 2.0) / ROT_DIM)
    ang = pos.astype(F32)[:, None] * inv[None, :]
    return jnp.cos(ang), jnp.sin(ang)


def _rope_tables(pos):
    cos, sin = _rope_angles(pos)
    reps = LANES // ROT_HALF
    cos, sin = jnp.tile(cos, (1, reps)), jnp.tile(sin, (1, reps))
    lane = lax.broadcasted_iota(jnp.int32, cos.shape, 1) % HEAD_DIM
    x1 = lane < ROT_HALF
    x2 = (lane >= ROT_HALF) & (lane < ROT_DIM)
    return (jnp.where(x1 | x2, cos, 1.0),
            jnp.where(x2, sin, 0.0),
            jnp.where(x1, -sin, 0.0))


def _rope(t, cos, sa, sb):
    blocks = []
    for b in range(t.shape[1] // LANES):
        blk = t[:, b * LANES:(b + 1) * LANES]
        blocks.append(blk * cos + pltpu.roll(blk, ROT_HALF, 1) * sa
                      + pltpu.roll(blk, LANES - ROT_HALF, 1) * sb)
    return jnp.concatenate(blocks, axis=1)


def _qkv_prompt_body(x_ref, g_ref, wq_ref, wkt_ref, wv_ref, cos_ref, sa_ref, sb_ref, cost_ref,
                     sint_ref, q_ref, kt_ref, ktb_ref, v_ref, vb_ref, *, tm):
    h = _rms(x_ref[...], g_ref[...], EPS).astype(BF16)
    q = jnp.dot(h, wq_ref[...], preferred_element_type=F32)
    v = jnp.dot(h, wv_ref[...], preferred_element_type=F32)
    q = _rope(q, cos_ref[...], sa_ref[...], sb_ref[...]) * (HEAD_DIM ** -0.5 * LOG2_E)
    q_ref[...] = q.astype(BF16)

    kt = lax.dot_general(wkt_ref[...], h, (((1,), (1,)), ((), ())), preferred_element_type=F32)

    vb_ref[...] = v.astype(BF16)
    for hd in range(N_HEADS):
        v_ref[pl.ds(hd, tm, stride=N_HEADS), :] = v[:, hd * V_DIM:(hd + 1) * V_DIM]

    c, s = cost_ref[...], sint_ref[...]
    pieces = []
    for sh in range(N_SUB):
        base = sh * HEAD_DIM
        x1 = kt[base:base + ROT_HALF, :]
        x2 = kt[base + ROT_HALF:base + ROT_DIM, :]
        pieces += [x1 * c - x2 * s, x2 * c + x1 * s, kt[base + ROT_DIM:base + HEAD_DIM, :]]
    kt = jnp.concatenate(pieces, axis=0)
    kt_ref[0] = kt
    ktb_ref[0, 0] = kt.astype(BF16)


def _qkv_prompt(x, g, wq, wkt, wv, tables, tables_t, *, n_seq, seq, tm):
    n = x.shape[0]
    tps = seq // tm
    row = pl.BlockSpec((tm, D_ATT), lambda i: (i, 0))
    tbl = pl.BlockSpec((tm, LANES), lambda i: (i % tps, 0))
    tbl_t = pl.BlockSpec((ROT_HALF, tm), lambda i: (0, i % tps))
    w_spec = _const_spec((D_MODEL, D_ATT))
    return pl.pallas_call(
        functools.partial(_qkv_prompt_body, tm=tm),
        out_shape=[jax.ShapeDtypeStruct((n, D_ATT), BF16),
                   jax.ShapeDtypeStruct((n_seq, D_ATT, seq), F32),
                   jax.ShapeDtypeStruct((n_seq, tps, D_ATT, tm), BF16),
                   jax.ShapeDtypeStruct((n * N_HEADS, V_DIM), F32),
                   jax.ShapeDtypeStruct((n, D_ATT), BF16)],
        grid=(n // tm,),
        in_specs=[pl.BlockSpec((tm, D_MODEL), lambda i: (i, 0)), _const_spec((1, D_MODEL)),
                  w_spec, w_spec, w_spec, tbl, tbl, tbl, tbl_t, tbl_t],
        out_specs=[row,
                   pl.BlockSpec((1, D_ATT, tm), lambda i: (i // tps, 0, i % tps)),
                   pl.BlockSpec((1, 1, D_ATT, tm), lambda i: (i // tps, i % tps, 0, 0)),
                   pl.BlockSpec((tm * N_HEADS, V_DIM), lambda i: (i, 0)),
                   row],
        compiler_params=_params(1), name="qkv_prompt")(
            x, g.reshape(1, D_MODEL), wq, wkt, wv, *tables, *tables_t)


def _qkv_decode_body(x_ref, g_ref, w_ref, cos_ref, sa_ref, sb_ref, q_ref, k_ref, v_ref):
    h = _rms(x_ref[...], g_ref[...], EPS).astype(BF16)
    qkv = jnp.dot(h, w_ref[...], preferred_element_type=F32)
    cos, sa, sb = cos_ref[...], sa_ref[...], sb_ref[...]
    q_ref[...] = _rope(qkv[:, 0:D_ATT], cos, sa, sb) * (HEAD_DIM ** -0.5)
    k_ref[...] = _rope(qkv[:, D_ATT:2 * D_ATT], cos, sa, sb)
    v_ref[...] = qkv[:, 2 * D_ATT:]


def _qkv_decode(x, g, w, tables):
    n = x.shape[0]
    full = lambda shape: pl.BlockSpec(shape, lambda i: (0,) * len(shape))
    out = jax.ShapeDtypeStruct((n, D_ATT), F32)
    return pl.pallas_call(
        _qkv_decode_body, out_shape=[out, out, out], grid=(1,),
        in_specs=[full((n, D_MODEL)), full((1, D_MODEL)), full((D_MODEL, 3 * D_ATT)),
                  full((n, LANES)), full((n, LANES)), full((n, LANES))],
        out_specs=[full((n, D_ATT))] * 3,
        compiler_params=_params(1), name="qkv_decode")(x, g.reshape(1, D_MODEL), w, *tables)


def _lambda(lq, lam_init):
    a = jnp.sum(lq[0:1, :] * lq[1:2, :], axis=-1, keepdims=True)
    b = jnp.sum(lq[2:3, :] * lq[3:4, :], axis=-1, keepdims=True)
    return jnp.exp(a) - jnp.exp(b) + lam_init


def _attn_body(pt_ref, lq_ref, sg_ref, q_ref, kt_ref, v_ref, qd_ref, kn_ref, vn_ref, *refs,
               tq, rb_full, rb_diag, ck, page, pages, dec_steps, lam_init):
    del pt_ref
    k_refs = refs[:pages]
    v_refs = refs[pages:2 * pages]
    (o_ref, od_ref, q2_scr, m_scr, l_scr, acc_scr,
     qx_scr, sd_scr, pd_scr, md_scr, ld_scr, al_scr, accd_scr) = refs[2 * pages:]
    qi = pl.program_id(2)
    step_id = (pl.program_id(0) * pl.num_programs(1) + pl.program_id(1)) * pl.num_programs(2) + qi
    jd = lax.rem(step_id, dec_steps)
    sub = lax.broadcasted_iota(jnp.int32, (N_SUB, D_ATT), 0)
    lane_d = lax.broadcasted_iota(jnp.int32, (N_SUB, D_ATT), 1)

    def q_expanded():
        return jnp.where(lax.shift_right_logical(lane_d, 6) == sub, qd_ref[0], 0.0)

    @pl.when(jd == 0)
    def _():
        qx_scr[...] = q_expanded().astype(BF16)
        md_scr[...] = jnp.full((N_SUB, 1), NEG_INF, F32)
        ld_scr[...] = jnp.zeros((N_SUB, 1), F32)
        accd_scr[...] = jnp.zeros((N_SUB, D_ATT), F32)

    def decode_scores(r):
        sd_scr[:, r * page:(r + 1) * page] = jnp.dot(
            qx_scr[...], k_refs[r][...].astype(BF16), preferred_element_type=F32)

    def decode_softmax():
        s = sd_scr[...]
        m_prev = md_scr[...]
        m_new = jnp.maximum(m_prev, jnp.max(s, axis=-1, keepdims=True))
        alpha = jnp.exp(m_prev - m_new)
        p = jnp.exp(s - m_new)
        ld_scr[...] = alpha * ld_scr[...] + jnp.sum(p, axis=-1, keepdims=True)
        pd_scr[...] = p.astype(BF16)
        al_scr[...] = alpha
        md_scr[...] = m_new

    pv_sum = []

    def decode_values(r):
        v_page = jnp.concatenate(
            [v_refs[r][pl.ds(hd, page, stride=N_HEADS), :] for hd in range(N_HEADS)], axis=1)
        d = jnp.dot(pd_scr[:, r * page:(r + 1) * page], v_page.astype(BF16),
                    preferred_element_type=F32)
        pv_sum[:] = [d if not pv_sum else pv_sum[0] + d]

    def decode_accumulate():
        accd_scr[...] = al_scr[...] * accd_scr[...] + pv_sum[0]

    n_rb_diag = 2 * tq // rb_diag
    half = n_rb_diag // 2
    per_block = pages // half
    pieces = [[] for _ in range(n_rb_diag)]
    for r in range(pages):
        pieces[r // per_block].append(functools.partial(decode_scores, r))
        pieces[half + r // per_block].append(functools.partial(decode_values, r))
    pieces[half - 1].append(decode_softmax)
    pieces[n_rb_diag - 1].append(decode_accumulate)

    q = q_ref[0]
    lane = lax.broadcasted_iota(jnp.int32, (tq, V_DIM), 1)
    zero = jnp.zeros_like(q)
    q2_scr[0:tq, :] = jnp.where(lane < HEAD_DIM, q, zero)
    q2_scr[tq:2 * tq, :] = jnp.where(lane >= HEAD_DIM, q, zero)
    m_scr[...] = jnp.full((2 * tq, LANES), NEG_INF, F32)
    l_scr[...] = jnp.zeros((2 * tq, LANES), F32)
    acc_scr[...] = jnp.zeros((2 * tq, V_DIM), F32)

    def step(j, diagonal):
        start = pl.multiple_of(j * tq, tq)
        rb = rb_diag if diagonal else rb_full
        n_rb = 2 * tq // rb

        def n_keys(r):
            return (r * rb) % tq + rb if diagonal else tq

        def scores(r):
            n_k = n_keys(r)
            cols = [kt_ref[0, j * (tq // ck) + c // ck, :, 0:min(ck, n_k - c)]
                    for c in range(0, n_k, ck)]
            kt = cols[0] if len(cols) == 1 else jnp.concatenate(cols, axis=1)
            return jnp.dot(q2_scr[r * rb:(r + 1) * rb, :], kt,
                           preferred_element_type=F32)

        pending = [scores(r) for r in range(min(QK_AHEAD, n_rb))]
        for r in range(n_rb):
            rows = slice(r * rb, (r + 1) * rb)
            first = (r * rb) % tq
            n_k = n_keys(r)
            vc = v_ref[0, pl.ds(start, n_k), :]
            s = pending.pop(0)
            if r + QK_AHEAD < n_rb:
                pending.append(scores(r + QK_AHEAD))
            if diagonal:
                r_i = lax.broadcasted_iota(jnp.int32, (rb, rb), 0)
                c_i = lax.broadcasted_iota(jnp.int32, (rb, rb), 1)
                tri = jnp.where(c_i <= r_i, s[:, first:], NEG_INF)
                s = tri if first == 0 else jnp.concatenate([s[:, :first], tri], axis=1)
            m_prev = m_scr[rows, :]
            m_new = jnp.maximum(m_prev, jnp.max(s, axis=-1, keepdims=True))
            alpha = jnp.exp2(m_prev - m_new)
            p = jnp.exp2(s - jnp.tile(m_new, (1, n_k // LANES)))
            l_scr[rows, :] = alpha * l_scr[rows, :] + jnp.sum(p, axis=-1, keepdims=True)
            acc_scr[rows, :] = alpha * acc_scr[rows, :] + jnp.dot(
                p.astype(BF16), vc, preferred_element_type=F32)
            m_scr[rows, :] = m_new
            if diagonal:
                for piece in pieces[r]:
                    piece()

    def loop_body(j, carry):
        step(j, False)
        return carry

    lax.fori_loop(0, qi, loop_body, 0)
    step(qi, True)

    o2 = acc_scr[...] / l_scr[...]
    lam = _lambda(lq_ref[...], lam_init)
    o = o2[0:tq, :] - lam * o2[tq:2 * tq, :]
    o = _rms(o, sg_ref[...], SUBLN_EPS) * (1.0 - lam_init)
    o_ref[0] = o.astype(o_ref.dtype)

    @pl.when(jd == dec_steps - 1)
    def _():
        s_new = jnp.sum(q_expanded() * kn_ref[0], axis=-1, keepdims=True)
        m_prev = md_scr[...]
        m_fin = jnp.maximum(m_prev, s_new)
        alpha = jnp.exp(m_prev - m_fin)
        p_new = jnp.exp(s_new - m_fin)
        l_fin = alpha * ld_scr[...] + p_new
        a = (alpha * accd_scr[...] + p_new * vn_ref[0]) / l_fin
        w = jnp.where((sub & 1) == 0, 1.0, -lam)
        own = lax.shift_right_logical(lane_d, 7) == lax.shift_right_logical(sub, 1)
        od_ref[0] = jnp.sum(jnp.where(own, a * w, 0.0), axis=0, keepdims=True)


def _attention(q, ktb, vb, qd, kd, vd, cache_k, cache_v, page_table, lq, sg, *, n_seq, seq, tq,
               rb_full, rb_diag, lam_init):
    ck = ktb.shape[3]
    n_dec = qd.shape[0]
    n_phys, page = cache_k.shape[0], cache_k.shape[1]
    n_pages = page_table.shape[1]
    n_q = seq // tq
    n_steps = n_seq * N_HEADS * n_q
    dec_steps = n_steps // n_dec
    pages = n_pages // dec_steps
    assert dec_steps * n_dec == n_steps and pages * dec_steps == n_pages
    assert pages % (tq // rb_diag) == 0
    q3 = q.reshape(n_seq, seq, D_ATT)
    v3 = vb.reshape(n_seq, seq, D_ATT)
    c_k = jnp.transpose(cache_k, (0, 2, 3, 1)).reshape(n_phys, D_ATT, page)
    c_v = cache_v.reshape(n_phys, page * N_HEADS, V_DIM)

    def lin(b, h, i):
        return (b * N_HEADS + h) * n_q + i

    q_spec = pl.BlockSpec((1, tq, V_DIM), lambda b, h, i, pt: (b, i, h))
    d_spec = pl.BlockSpec((1, 1, D_ATT), lambda b, h, i, pt: (lin(b, h, i) // dec_steps, 0, 0))

    def page_spec(shape, r):
        def index(b, h, i, pt):
            t = lin(b, h, i)
            return (pt[t // dec_steps, (t % dec_steps) * pages + r], 0, 0)
        return pl.BlockSpec((None,) + shape, index)

    grid_spec = pltpu.PrefetchScalarGridSpec(
        num_scalar_prefetch=1, grid=(n_seq, N_HEADS, n_q),
        in_specs=[pl.BlockSpec((4, HEAD_DIM), lambda b, h, i, pt: (0, 0)),
                  pl.BlockSpec((1, V_DIM), lambda b, h, i, pt: (0, 0)), q_spec,
                  pl.BlockSpec((1, seq // ck, V_DIM, ck), lambda b, h, i, pt: (b, 0, h, 0)),
                  pl.BlockSpec((1, seq, V_DIM), lambda b, h, i, pt: (b, 0, h)),
                  d_spec, d_spec, d_spec]
        + [page_spec((D_ATT, page), r) for r in range(pages)]
        + [page_spec((page * N_HEADS, V_DIM), r) for r in range(pages)],
        out_specs=[q_spec, d_spec],
        scratch_shapes=[pltpu.VMEM((2 * tq, V_DIM), BF16), pltpu.VMEM((2 * tq, LANES), F32),
                        pltpu.VMEM((2 * tq, LANES), F32), pltpu.VMEM((2 * tq, V_DIM), F32),
                        pltpu.VMEM((N_SUB, D_ATT), BF16), pltpu.VMEM((N_SUB, pages * page), F32),
                        pltpu.VMEM((N_SUB, pages * page), BF16), pltpu.VMEM((N_SUB, 1), F32),
                        pltpu.VMEM((N_SUB, 1), F32), pltpu.VMEM((N_SUB, 1), F32),
                        pltpu.VMEM((N_SUB, D_ATT), F32)])
    out, out_d = pl.pallas_call(
        functools.partial(_attn_body, tq=tq, rb_full=rb_full, rb_diag=rb_diag, ck=ck, page=page,
                          pages=pages, dec_steps=dec_steps, lam_init=lam_init),
        out_shape=[jax.ShapeDtypeStruct((n_seq, seq, D_ATT), BF16),
                   jax.ShapeDtypeStruct((n_dec, 1, D_ATT), F32)],
        grid_spec=grid_spec, compiler_params=_params(3), name="attention")(
            page_table, lq, sg.reshape(1, V_DIM), q3, ktb, v3, qd.reshape(n_dec, 1, D_ATT),
            kd.reshape(n_dec, 1, D_ATT), vd.reshape(n_dec, 1, D_ATT),
            *([c_k] * pages), *([c_v] * pages))
    return out.reshape(n_seq * seq, D_ATT), out_d.reshape(n_dec, D_ATT)


def _attn_out_decode_body(x_ref, a_ref, w_ref, sg_ref, out_ref, *, lam_init):
    a = a_ref[...]
    sg = sg_ref[...]
    blocks = []
    for hd in range(N_HEADS):
        blk = a[:, hd * V_DIM:(hd + 1) * V_DIM]
        blocks.append(_rms(blk, sg, SUBLN_EPS) * (1.0 - lam_init))
    a = jnp.concatenate(blocks, axis=1).astype(BF16)
    out_ref[...] = x_ref[...] + jnp.dot(a, w_ref[...], preferred_element_type=F32)


def _attn_out_decode(x, a, w, sg, *, lam_init):
    n = x.shape[0]
    full = lambda shape: pl.BlockSpec(shape, lambda i: (0,) * len(shape))
    return pl.pallas_call(
        functools.partial(_attn_out_decode_body, lam_init=lam_init),
        out_shape=jax.ShapeDtypeStruct((n, D_MODEL), F32), grid=(1,),
        in_specs=[full((n, D_MODEL)), full((n, D_ATT)), full((D_ATT, D_MODEL)), full((1, V_DIM))],
        out_specs=full((n, D_MODEL)),
        compiler_params=_params(1), name="attn_out_decode")(x, a, w, sg.reshape(1, V_DIM))


def kernel(x_prompt, x_sample, cache_conv, cache_k, cache_v, page_table, norm_g, ffn_w_gate,
           ffn_w_up, ffn_w_down, w_in_mix, conv_w, sg_ln_g, sg_ln_b, sg_w, sg_b, w_out_mix, w_qkv,
           lambda_qk, subln_g, w_o, final_norm_g):
    n_seq, seq, _ = x_prompt.shape
    n_dec, dec_seq, _ = x_sample.shape
    tm = 512
    assert dec_seq == 1 and seq % tm == 0
    depth = norm_g.shape[0]
    past_len = page_table.shape[1] * cache_k.shape[2]
    assert past_len % CHUNK == 0

    wg, wu, wd = ffn_w_gate.astype(BF16), ffn_w_up.astype(BF16), ffn_w_down.astype(BF16)
    win, wout = w_in_mix.astype(BF16), w_out_mix.astype(BF16)
    wqkv, wo = w_qkv.astype(BF16), w_o.astype(BF16)

    xp = x_prompt.reshape(n_seq * seq, D_MODEL)
    xs = x_sample.reshape(n_dec, D_MODEL)
    ffn = functools.partial(_ffn, tm=2 * tm, ff_chunk=256)
    pos_p = jnp.arange(seq)
    pos_s = jnp.broadcast_to(past_len + jnp.arange(dec_seq), (n_dec,))
    tables_p, tables_s = _rope_tables(pos_p), _rope_tables(pos_s)
    tables_pt = tuple(a.T for a in _rope_angles(pos_p))

    conv_p, conv_s, chunk_s, k_p, v_p, k_s, v_s = [], [], [], [], [], [], []
    attn_p = None
    for l in range(depth):
        last = l == depth - 1
        xp, xs = ffn(xp, xs, norm_g[l, 0], wg, wu, wd, (l, 0))
        if l % 2 == 0:
            i = l // 2
            sgb_tbl = jnp.broadcast_to(sg_b[i][:, :, None], (G_B, CHUNK, LANES))
            xp, st = _mix_prompt(xp, norm_g[l, 1], win[i], conv_w[i], sg_ln_g[i], sg_ln_b[i],
                                 sg_w[i], sgb_tbl, wout[i], n_seq=n_seq, seq=seq, tm=2 * tm)
            conv_p.append(st)
            w00 = jnp.repeat(sg_w[i][:, 0, 0], D_B // G_B).reshape(1, D_B)
            b0 = jnp.repeat(sg_b[i][:, 0], D_B // G_B).reshape(1, D_B)
            xs, st, vn = _mix_decode(xs, norm_g[l, 1], win[i], conv_w[i],
                                     cache_conv[i].reshape(n_dec, (CONV_W - 1) * D_A),
                                     sg_ln_g[i], sg_ln_b[i], w00, b0, wout[i])
            conv_s.append(st.reshape(n_dec, CONV_W - 1, D_A))
            chunk_s.append(vn.reshape(n_dec, dec_seq, D_B))
        else:
            j = l // 2
            lam_init = 0.8 - 0.6 * math.exp(-0.3 * l)
            wq = wqkv[j][:, 0:D_ATT]
            wkt = wqkv[j][:, D_ATT:2 * D_ATT].T
            wv = wqkv[j][:, 2 * D_ATT:]
            q, kt, ktb, v, vb = _qkv_prompt(xp, norm_g[l, 1], wq, wkt, wv, tables_p, tables_pt,
                                            n_seq=n_seq, seq=seq, tm=tm)
            qd, kd, vd = _qkv_decode(xs, norm_g[l, 1], wqkv[j], tables_s)
            a, ad = _attention(q, ktb, vb, qd, kd, vd, cache_k[j], cache_v[j], page_table,
                               lambda_qk[j], subln_g[j], n_seq=n_seq, seq=seq, tq=1024,
                               rb_full=512, rb_diag=256, lam_init=lam_init)
            attn_p = (a, wo[j])
            xs = _attn_out_decode(xs, ad, wo[j], subln_g[j], lam_init=lam_init)
            k_p.append(jnp.transpose(kt.reshape(n_seq, N_SUB, HEAD_DIM, seq), (0, 3, 1, 2)))
            v_p.append(v.reshape(n_seq, seq, N_HEADS, V_DIM))
            k_s.append(kd.reshape(n_dec, dec_seq, N_SUB, HEAD_DIM))
            v_s.append(vd.reshape(n_dec, dec_seq, N_HEADS, V_DIM))
        fg = final_norm_g if last else None
        if attn_p is None:
            xp, xs = ffn(xp, xs, norm_g[l, 2], wg, wu, wd, (l, 1), final_g=fg)
        else:
            xp, xs = ffn(xp, xs, norm_g[l, 2], wg, wu, wd, (l, 1), attn=attn_p[0],
                         w_o=attn_p[1], final_g=fg)
            attn_p = None

    return (xp.reshape(n_seq, seq, D_MODEL), xs.reshape(n_dec, dec_seq, D_MODEL),
            jnp.stack(conv_p), jnp.stack(conv_s), jnp.stack(chunk_s),
            jnp.stack(k_p), jnp.stack(v_p), jnp.stack(k_s), jnp.stack(v_s))
```

```python
import functools
import math

import jax
import jax.numpy as jnp
from jax import lax
from jax.experimental import pallas as pl
from jax.experimental.pallas import tpu as pltpu

F32 = jnp.float32
BF16 = jnp.bfloat16

D_MODEL = 1024
D_A = D_MODEL // 2
D_B = D_MODEL // 2
CONV_W = 3
CHUNK = 128
G_B = 4
N_HEADS = 8
HEAD_DIM = 64
N_SUB = 2 * N_HEADS
V_DIM = 2 * HEAD_DIM
D_ATT = N_SUB * HEAD_DIM
ROT_DIM = HEAD_DIM // 4
ROT_HALF = ROT_DIM // 2
ROPE_THETA = 500000.0
D_FF = 2816
EPS = 1e-6
SUBLN_EPS = 1e-5
LN_EPS = 1e-5
NEG_INF = -1e30
LOG2_E = math.log2(math.e)

LANES = 128
SUBLANES = 8
VMEM_LIMIT = 56 * 1024 * 1024
MIX_SLAB = 256
QK_AHEAD = 1


def _rms(x, g, eps):
    ms = jnp.mean(x * x, axis=-1, keepdims=True)
    return x * lax.rsqrt(ms + eps) * g


def _const_spec(shape):
    return pl.BlockSpec(shape, lambda *_: (0,) * len(shape), pipeline_mode=pl.Buffered(1))


def _stacked_spec(shape, index):
    return pl.BlockSpec((None,) * len(index) + shape,
                        lambda *_: tuple(index) + (0,) * len(shape),
                        pipeline_mode=pl.Buffered(1))


def _params(n_axes):
    return pltpu.CompilerParams(dimension_semantics=("arbitrary",) * n_axes,
                                vmem_limit_bytes=VMEM_LIMIT)


def _ffn_body(x_ref, xs_ref, g_ref, wg_ref, wu_ref, wd_ref, *rest, ff_chunk, has_attn, has_final):
    rest = list(rest)
    os_ref = rest.pop()
    o_ref = rest.pop()
    final_g = rest.pop()[...] if has_final else None

    def ffn_rows(x):
        h = _rms(x, g_ref[...], EPS).astype(BF16)
        acc = None
        for c in range(D_FF // ff_chunk):
            sl = slice(c * ff_chunk, (c + 1) * ff_chunk)
            gate = jnp.dot(h, wg_ref[:, sl].astype(BF16), preferred_element_type=F32)
            up = jnp.dot(h, wu_ref[:, sl].astype(BF16), preferred_element_type=F32)
            a = (gate * jax.nn.sigmoid(gate) * up).astype(BF16)
            d = jnp.dot(a, wd_ref[sl, :].astype(BF16), preferred_element_type=F32)
            acc = d if acc is None else acc + d
        y = x + 0.5 * acc
        return _rms(y, final_g, EPS) if has_final else y

    x = x_ref[...]
    if has_attn:
        a_ref, wo_ref = rest
        x = x + jnp.dot(a_ref[...], wo_ref[...], preferred_element_type=F32)
    o_ref[...] = ffn_rows(x)

    @pl.when(pl.program_id(0) == pl.num_programs(0) - 1)
    def _():
        os_ref[...] = ffn_rows(xs_ref[...])


def _ffn(x, xs, g, wg, wu, wd, idx, *, tm, ff_chunk, attn=None, w_o=None, final_g=None):
    n, ns = x.shape[0], xs.shape[0]
    row = pl.BlockSpec((tm, D_MODEL), lambda i: (i, 0))
    side = pl.BlockSpec((ns, D_MODEL), lambda i: (0, 0))
    in_specs = [row, side, _const_spec((1, D_MODEL)), _stacked_spec((D_MODEL, D_FF), idx),
                _stacked_spec((D_MODEL, D_FF), idx), _stacked_spec((D_FF, D_MODEL), idx)]
    args = [x, xs, g.reshape(1, D_MODEL), wg, wu, wd]
    if attn is not None:
        in_specs += [pl.BlockSpec((tm, D_ATT), lambda i: (i, 0)), _const_spec((D_ATT, D_MODEL))]
        args += [attn, w_o]
    if final_g is not None:
        in_specs.append(_const_spec((1, D_MODEL)))
        args.append(final_g.reshape(1, D_MODEL))
    return pl.pallas_call(
        functools.partial(_ffn_body, ff_chunk=ff_chunk, has_attn=attn is not None,
                          has_final=final_g is not None),
        out_shape=[jax.ShapeDtypeStruct((n, D_MODEL), F32),
                   jax.ShapeDtypeStruct((ns, D_MODEL), F32)],
        grid=(n // tm,), in_specs=in_specs, out_specs=[row, side],
        compiler_params=_params(1), name="ffn")(*args)


def _layer_norm(v, g, b):
    mu = jnp.mean(v, axis=-1, keepdims=True)
    vc = v - mu
    var = jnp.mean(vc * vc, axis=-1, keepdims=True)
    return vc * lax.rsqrt(var + LN_EPS) * g + b


def _mix_prompt_body(x_ref, g_ref, win_ref, cw_ref, lng_ref, lnb_ref, sgw_ref, sgb_ref, wout_ref,
                     o_ref, st_ref, xc_scr, *, tm, tiles_per_seq, parts):
    i = pl.program_id(0)
    pm = tm // parts
    n_chunks = pm // CHUNK
    r_i = lax.broadcasted_iota(jnp.int32, (CHUNK, CHUNK), 0)
    c_i = lax.broadcasted_iota(jnp.int32, (CHUNK, CHUNK), 1)
    w_tril = [jnp.where(c_i <= r_i, sgw_ref[gi], 0.0).astype(BF16) for gi in range(G_B)]
    cw = cw_ref[...]

    def in_proj(k):
        h = _rms(x_ref[k * pm:(k + 1) * pm, :], g_ref[...], EPS).astype(BF16)
        return jnp.dot(h, win_ref[...], preferred_element_type=F32)

    @pl.when(i % tiles_per_seq == 0)
    def _():
        xc_scr[0:SUBLANES, :] = jnp.zeros((SUBLANES, D_A), F32)

    proj_next = in_proj(0)
    for k in range(parts):
        proj = proj_next
        if k + 1 < parts:
            proj_next = in_proj(k + 1)
        bg = proj[:, 0:D_A]
        cg = proj[:, D_A:2 * D_A]
        hx = proj[:, 2 * D_A:3 * D_A]
        u = proj[:, 3 * D_A:3 * D_A + D_B]
        v = proj[:, 3 * D_A + D_B:]

        xc = cg * hx
        base = SUBLANES + k * pm
        xc_scr[base:base + pm, :] = xc
        xm1 = xc_scr[base - 1:base - 1 + pm, :]
        xm2 = xc_scr[base - 2:base - 2 + pm, :]
        ya = bg * (cw[0:1, :] * xm2 + cw[1:2, :] * xm1 + cw[2:3, :] * xc)
        if k == parts - 1:
            st_ref[0] = xc[pm - (CONV_W - 1):pm, :]

        vn = _layer_norm(v, lng_ref[...], lnb_ref[...]).astype(BF16)
        zs = []
        for gi in range(G_B):
            vg = vn[:, gi * LANES:(gi + 1) * LANES]
            wide = jnp.concatenate([vg[c * CHUNK:(c + 1) * CHUNK, :] for c in range(n_chunks)],
                                   axis=1)
            zw = jnp.dot(w_tril[gi], wide, preferred_element_type=F32)
            bias = sgb_ref[gi]
            zs.append(jnp.concatenate(
                [zw[:, c * CHUNK:(c + 1) * CHUNK] + bias for c in range(n_chunks)], axis=0))
        yb = u * jnp.concatenate(zs, axis=1)

        mixin = jnp.concatenate([ya, yb], axis=1).astype(BF16)
        o_ref[k * pm:(k + 1) * pm, :] = x_ref[k * pm:(k + 1) * pm, :] + jnp.dot(
            mixin, wout_ref[...], preferred_element_type=F32)

    xc_scr[0:SUBLANES, :] = xc_scr[tm:tm + SUBLANES, :]


def _mix_prompt(x, g, win, cw, lng, lnb, sgw, sgb_tbl, wout, *, n_seq, seq, tm):
    n = x.shape[0]
    tiles_per_seq = seq // tm
    d_in = 3 * D_A + 2 * D_B
    row = pl.BlockSpec((tm, D_MODEL), lambda i: (i, 0))
    in_specs = [row, _const_spec((1, D_MODEL)), _const_spec((D_MODEL, d_in)),
                _const_spec((CONV_W, D_A)), _const_spec((1, D_B)), _const_spec((1, D_B)),
                _const_spec((G_B, CHUNK, CHUNK)), _const_spec((G_B, CHUNK, LANES)),
                _const_spec((D_A + D_B, D_MODEL))]
    out_specs = [row, pl.BlockSpec((1, CONV_W - 1, D_A), lambda i: (i // tiles_per_seq, 0, 0))]
    return pl.pallas_call(
        functools.partial(_mix_prompt_body, tm=tm, tiles_per_seq=tiles_per_seq,
                          parts=tm // MIX_SLAB),
        out_shape=[jax.ShapeDtypeStruct((n, D_MODEL), F32),
                   jax.ShapeDtypeStruct((n_seq, CONV_W - 1, D_A), F32)],
        grid=(n // tm,), in_specs=in_specs, out_specs=out_specs,
        scratch_shapes=[pltpu.VMEM((tm + SUBLANES, D_A), F32)],
        compiler_params=_params(1), name="mix_prompt")(
            x, g.reshape(1, D_MODEL), win, cw, lng.reshape(1, D_B), lnb.reshape(1, D_B),
            sgw, sgb_tbl, wout)


def _mix_decode_body(x_ref, g_ref, win_ref, cw_ref, past_ref, lng_ref, lnb_ref, w00_ref, b0_ref,
                     wout_ref, o_ref, st_ref, vn_ref):
    x = x_ref[...]
    h = _rms(x, g_ref[...], EPS).astype(BF16)
    proj = jnp.dot(h, win_ref[...], preferred_element_type=F32)
    bg = proj[:, 0:D_A]
    cg = proj[:, D_A:2 * D_A]
    hx = proj[:, 2 * D_A:3 * D_A]
    u = proj[:, 3 * D_A:3 * D_A + D_B]
    v = proj[:, 3 * D_A + D_B:]
    xc = cg * hx
    p0 = past_ref[:, 0:D_A]
    p1 = past_ref[:, D_A:2 * D_A]
    cw = cw_ref[...]
    ya = bg * (cw[0:1, :] * p0 + cw[1:2, :] * p1 + cw[2:3, :] * xc)
    st_ref[:, 0:D_A] = p1
    st_ref[:, D_A:2 * D_A] = xc
    vn = _layer_norm(v, lng_ref[...], lnb_ref[...])
    vn_ref[...] = vn
    yb = u * (w00_ref[...] * vn + b0_ref[...])
    mixin = jnp.concatenate([ya, yb], axis=1).astype(BF16)
    o_ref[...] = x + jnp.dot(mixin, wout_ref[...], preferred_element_type=F32)


def _mix_decode(x, g, win, cw, past, lng, lnb, w00, b0, wout):
    n = x.shape[0]
    d_in = 3 * D_A + 2 * D_B
    full = lambda shape: pl.BlockSpec(shape, lambda i: (0,) * len(shape))
    return pl.pallas_call(
        _mix_decode_body,
        out_shape=[jax.ShapeDtypeStruct((n, D_MODEL), F32),
                   jax.ShapeDtypeStruct((n, (CONV_W - 1) * D_A), F32),
                   jax.ShapeDtypeStruct((n, D_B), F32)],
        grid=(1,),
        in_specs=[full((n, D_MODEL)), full((1, D_MODEL)), full((D_MODEL, d_in)),
                  full((CONV_W, D_A)), full((n, (CONV_W - 1) * D_A)), full((1, D_B)),
                  full((1, D_B)), full((1, D_B)), full((1, D_B)), full((D_A + D_B, D_MODEL))],
        out_specs=[full((n, D_MODEL)), full((n, (CONV_W - 1) * D_A)), full((n, D_B))],
        compiler_params=_params(1), name="mix_decode")(
            x, g.reshape(1, D_MODEL), win, cw, past, lng.reshape(1, D_B), lnb.reshape(1, D_B),
            w00, b0, wout)


def _rope_angles(pos):
    inv = ROPE_THETA ** (-(jnp.arange(ROT_HALF, dtype=F32) * 2.0) / ROT_DIM)
    ang = pos.astype(F32)[:, None] * inv[None, :]
    return jnp.cos(ang), jnp.sin(ang)


def _rope_tables(pos):
    cos, sin = _rope_angles(pos)
    reps = LANES // ROT_HALF
    cos, sin = jnp.tile(cos, (1, reps)), jnp.tile(sin, (1, reps))
    lane = lax.broadcasted_iota(jnp.int32, cos.shape, 1) % HEAD_DIM
    x1 = lane < ROT_HALF
    x2 = (lane >= ROT_HALF) & (lane < ROT_DIM)
    return (jnp.where(x1 | x2, cos, 1.0),
            jnp.where(x2, sin, 0.0),
            jnp.where(x1, -sin, 0.0))


def _rope(t, cos, sa, sb):
    blocks = []
    for b in range(t.shape[1] // LANES):
        blk = t[:, b * LANES:(b + 1) * LANES]
        blocks.append(blk * cos + pltpu.roll(blk, ROT_HALF, 1) * sa
                      + pltpu.roll(blk, LANES - ROT_HALF, 1) * sb)
    return jnp.concatenate(blocks, axis=1)


def _qkv_prompt_body(x_ref, g_ref, wq_ref, wkt_ref, wv_ref, cos_ref, sa_ref, sb_ref, cost_ref,
                     sint_ref, q_ref, kt_ref, ktb_ref, v_ref, vb_ref, *, tm):
    h = _rms(x_ref[...], g_ref[...], EPS).astype(BF16)
    q = jnp.dot(h, wq_ref[...], preferred_element_type=F32)
    v = jnp.dot(h, wv_ref[...], preferred_element_type=F32)
    q = _rope(q, cos_ref[...], sa_ref[...], sb_ref[...]) * (HEAD_DIM ** -0.5 * LOG2_E)
    q_ref[...] = q.astype(BF16)

    kt = lax.dot_general(wkt_ref[...], h, (((1,), (1,)), ((), ())), preferred_element_type=F32)

    vb_ref[...] = v.astype(BF16)
    for hd in range(N_HEADS):
        v_ref[pl.ds(hd, tm, stride=N_HEADS), :] = v[:, hd * V_DIM:(hd + 1) * V_DIM]

    c, s = cost_ref[...], sint_ref[...]
    pieces = []
    for sh in range(N_SUB):
        base = sh * HEAD_DIM
        x1 = kt[base:base + ROT_HALF, :]
        x2 = kt[base + ROT_HALF:base + ROT_DIM, :]
        pieces += [x1 * c - x2 * s, x2 * c + x1 * s, kt[base + ROT_DIM:base + HEAD_DIM, :]]
    kt = jnp.concatenate(pieces, axis=0)
    kt_ref[0] = kt
    ktb_ref[0, 0] = kt.astype(BF16)


def _qkv_prompt(x, g, wq, wkt, wv, tables, tables_t, *, n_seq, seq, tm):
    n = x.shape[0]
    tps = seq // tm
    row = pl.BlockSpec((tm, D_ATT), lambda i: (i, 0))
    tbl = pl.BlockSpec((tm, LANES), lambda i: (i % tps, 0))
    tbl_t = pl.BlockSpec((ROT_HALF, tm), lambda i: (0, i % tps))
    w_spec = _const_spec((D_MODEL, D_ATT))
    return pl.pallas_call(
        functools.partial(_qkv_prompt_body, tm=tm),
        out_shape=[jax.ShapeDtypeStruct((n, D_ATT), BF16),
                   jax.ShapeDtypeStruct((n_seq, D_ATT, seq), F32),
                   jax.ShapeDtypeStruct((n_seq, tps, D_ATT, tm), BF16),
                   jax.ShapeDtypeStruct((n * N_HEADS, V_DIM), F32),
                   jax.ShapeDtypeStruct((n, D_ATT), BF16)],
        grid=(n // tm,),
        in_specs=[pl.BlockSpec((tm, D_MODEL), lambda i: (i, 0)), _const_spec((1, D_MODEL)),
                  w_spec, w_spec, w_spec, tbl, tbl, tbl, tbl_t, tbl_t],
        out_specs=[row,
                   pl.BlockSpec((1, D_ATT, tm), lambda i: (i // tps, 0, i % tps)),
                   pl.BlockSpec((1, 1, D_ATT, tm), lambda i: (i // tps, i % tps, 0, 0)),
                   pl.BlockSpec((tm * N_HEADS, V_DIM), lambda i: (i, 0)),
                   row],
        compiler_params=_params(1), name="qkv_prompt")(
            x, g.reshape(1, D_MODEL), wq, wkt, wv, *tables, *tables_t)


def _qkv_decode_body(x_ref, g_ref, w_ref, cos_ref, sa_ref, sb_ref, q_ref, k_ref, v_ref):
    h = _rms(x_ref[...], g_ref[...], EPS).astype(BF16)
    qkv = jnp.dot(h, w_ref[...], preferred_element_type=F32)
    cos, sa, sb = cos_ref[...], sa_ref[...], sb_ref[...]
    q_ref[...] = _rope(qkv[:, 0:D_ATT], cos, sa, sb) * (HEAD_DIM ** -0.5)
    k_ref[...] = _rope(qkv[:, D_ATT:2 * D_ATT], cos, sa, sb)
    v_ref[...] = qkv[:, 2 * D_ATT:]


def _qkv_decode(x, g, w, tables):
    n = x.shape[0]
    full = lambda shape: pl.BlockSpec(shape, lambda i: (0,) * len(shape))
    out = jax.ShapeDtypeStruct((n, D_ATT), F32)
    return pl.pallas_call(
        _qkv_decode_body, out_shape=[out, out, out], grid=(1,),
        in_specs=[full((n, D_MODEL)), full((1, D_MODEL)), full((D_MODEL, 3 * D_ATT)),
                  full((n, LANES)), full((n, LANES)), full((n, LANES))],
        out_specs=[full((n, D_ATT))] * 3,
        compiler_params=_params(1), name="qkv_decode")(x, g.reshape(1, D_MODEL), w, *tables)


def _lambda(lq, lam_init):
    a = jnp.sum(lq[0:1, :] * lq[1:2, :], axis=-1, keepdims=True)
    b = jnp.sum(lq[2:3, :] * lq[3:4, :], axis=-1, keepdims=True)
    return jnp.exp(a) - jnp.exp(b) + lam_init


def _attn_body(pt_ref, lq_ref, sg_ref, q_ref, kt_ref, v_ref, qd_ref, kn_ref, vn_ref, *refs,
               tq, rb_full, rb_diag, ck, page, pages, dec_steps, lam_init):
    del pt_ref
    k_refs = refs[:pages]
    v_refs = refs[pages:2 * pages]
    (o_ref, od_ref, q2_scr, m_scr, l_scr, acc_scr,
     qx_scr, sd_scr, pd_scr, md_scr, ld_scr, al_scr, accd_scr) = refs[2 * pages:]
    qi = pl.program_id(2)
    step_id = (pl.program_id(0) * pl.num_programs(1) + pl.program_id(1)) * pl.num_programs(2) + qi
    jd = lax.rem(step_id, dec_steps)
    sub = lax.broadcasted_iota(jnp.int32, (N_SUB, D_ATT), 0)
    lane_d = lax.broadcasted_iota(jnp.int32, (N_SUB, D_ATT), 1)

    def q_expanded():
        return jnp.where(lax.shift_right_logical(lane_d, 6) == sub, qd_ref[0], 0.0)

    @pl.when(jd == 0)
    def _():
        qx_scr[...] = q_expanded().astype(BF16)
        md_scr[...] = jnp.full((N_SUB, 1), NEG_INF, F32)
        ld_scr[...] = jnp.zeros((N_SUB, 1), F32)
        accd_scr[...] = jnp.zeros((N_SUB, D_ATT), F32)

    def decode_scores(r):
        sd_scr[:, r * page:(r + 1) * page] = jnp.dot(
            qx_scr[...], k_refs[r][...].astype(BF16), preferred_element_type=F32)

    def decode_softmax():
        s = sd_scr[...]
        m_prev = md_scr[...]
        m_new = jnp.maximum(m_prev, jnp.max(s, axis=-1, keepdims=True))
        alpha = jnp.exp(m_prev - m_new)
        p = jnp.exp(s - m_new)
        ld_scr[...] = alpha * ld_scr[...] + jnp.sum(p, axis=-1, keepdims=True)
        pd_scr[...] = p.astype(BF16)
        al_scr[...] = alpha
        md_scr[...] = m_new

    pv_sum = []

    def decode_values(r):
        v_page = jnp.concatenate(
            [v_refs[r][pl.ds(hd, page, stride=N_HEADS), :] for hd in range(N_HEADS)], axis=1)
        d = jnp.dot(pd_scr[:, r * page:(r + 1) * page], v_page.astype(BF16),
                    preferred_element_type=F32)
        pv_sum[:] = [d if not pv_sum else pv_sum[0] + d]

    def decode_accumulate():
        accd_scr[...] = al_scr[...] * accd_scr[...] + pv_sum[0]

    n_rb_diag = 2 * tq // rb_diag
    half = n_rb_diag // 2
    per_block = pages // half
    pieces = [[] for _ in range(n_rb_diag)]
    for r in range(pages):
        pieces[r // per_block].append(functools.partial(decode_scores, r))
        pieces[half + r // per_block].append(functools.partial(decode_values, r))
    pieces[half - 1].append(decode_softmax)
    pieces[n_rb_diag - 1].append(decode_accumulate)

    q = q_ref[0]
    lane = lax.broadcasted_iota(jnp.int32, (tq, V_DIM), 1)
    zero = jnp.zeros_like(q)
    q2_scr[0:tq, :] = jnp.where(lane < HEAD_DIM, q, zero)
    q2_scr[tq:2 * tq, :] = jnp.where(lane >= HEAD_DIM, q, zero)
    m_scr[...] = jnp.full((2 * tq, LANES), NEG_INF, F32)
    l_scr[...] = jnp.zeros((2 * tq, LANES), F32)
    acc_scr[...] = jnp.zeros((2 * tq, V_DIM), F32)

    def step(j, diagonal):
        start = pl.multiple_of(j * tq, tq)
        rb = rb_diag if diagonal else rb_full
        n_rb = 2 * tq // rb

        def n_keys(r):
            return (r * rb) % tq + rb if diagonal else tq

        def scores(r):
            n_k = n_keys(r)
            cols = [kt_ref[0, j * (tq // ck) + c // ck, :, 0:min(ck, n_k - c)]
                    for c in range(0, n_k, ck)]
            kt = cols[0] if len(cols) == 1 else jnp.concatenate(cols, axis=1)
            return jnp.dot(q2_scr[r * rb:(r + 1) * rb, :], kt,
                           preferred_element_type=F32)

        pending = [scores(r) for r in range(min(QK_AHEAD, n_rb))]
        for r in range(n_rb):
            rows = slice(r * rb, (r + 1) * rb)
            first = (r * rb) % tq
            n_k = n_keys(r)
            vc = v_ref[0, pl.ds(start, n_k), :]
            s = pending.pop(0)
            if r + QK_AHEAD < n_rb:
                pending.append(scores(r + QK_AHEAD))
            if diagonal:
                r_i = lax.broadcasted_iota(jnp.int32, (rb, rb), 0)
                c_i = lax.broadcasted_iota(jnp.int32, (rb, rb), 1)
                tri = jnp.where(c_i <= r_i, s[:, first:], NEG_INF)
                s = tri if first == 0 else jnp.concatenate([s[:, :first], tri], axis=1)
            m_prev = m_scr[rows, :]
            m_new = jnp.maximum(m_prev, jnp.max(s, axis=-1, keepdims=True))
            alpha = jnp.exp2(m_prev - m_new)
            p = jnp.exp2(s - jnp.tile(m_new, (1, n_k // LANES)))
            l_scr[rows, :] = alpha * l_scr[rows, :] + jnp.sum(p, axis=-1, keepdims=True)
            acc_scr[rows, :] = alpha * acc_scr[rows, :] + jnp.dot(
                p.astype(BF16), vc, preferred_element_type=F32)
            m_scr[rows, :] = m_new
            if diagonal:
                for piece in pieces[r]:
                    piece()

    def loop_body(j, carry):
        step(j, False)
        return carry

    lax.fori_loop(0, qi, loop_body, 0)
    step(qi, True)

    o2 = acc_scr[...] / l_scr[...]
    lam = _lambda(lq_ref[...], lam_init)
    o = o2[0:tq, :] - lam * o2[tq:2 * tq, :]
    o = _rms(o, sg_ref[...], SUBLN_EPS) * (1.0 - lam_init)
    o_ref[0] = o.astype(o_ref.dtype)

    @pl.when(jd == dec_steps - 1)
    def _():
        s_new = jnp.sum(q_expanded() * kn_ref[0], axis=-1, keepdims=True)
        m_prev = md_scr[...]
        m_fin = jnp.maximum(m_prev, s_new)
        alpha = jnp.exp(m_prev - m_fin)
        p_new = jnp.exp(s_new - m_fin)
        l_fin = alpha * ld_scr[...] + p_new
        a = (alpha * accd_scr[...] + p_new * vn_ref[0]) / l_fin
        w = jnp.where((sub & 1) == 0, 1.0, -lam)
        own = lax.shift_right_logical(lane_d, 7) == lax.shift_right_logical(sub, 1)
        od_ref[0] = jnp.sum(jnp.where(own, a * w, 0.0), axis=0, keepdims=True)


def _attention(q, ktb, vb, qd, kd, vd, cache_k, cache_v, page_table, lq, sg, *, n_seq, seq, tq,
               rb_full, rb_diag, lam_init):
    ck = ktb.shape[3]
    n_dec = qd.shape[0]
    n_phys, page = cache_k.shape[0], cache_k.shape[1]
    n_pages = page_table.shape[1]
    n_q = seq // tq
    n_steps = n_seq * N_HEADS * n_q
    dec_steps = n_steps // n_dec
    pages = n_pages // dec_steps
    assert dec_steps * n_dec == n_steps and pages * dec_steps == n_pages
    assert pages % (tq // rb_diag) == 0
    q3 = q.reshape(n_seq, seq, D_ATT)
    v3 = vb.reshape(n_seq, seq, D_ATT)
    c_k = jnp.transpose(cache_k, (0, 2, 3, 1)).reshape(n_phys, D_ATT, page)
    c_v = cache_v.reshape(n_phys, page * N_HEADS, V_DIM)

    def lin(b, h, i):
        return (b * N_HEADS + h) * n_q + i

    q_spec = pl.BlockSpec((1, tq, V_DIM), lambda b, h, i, pt: (b, i, h))
    d_spec = pl.BlockSpec((1, 1, D_ATT), lambda b, h, i, pt: (lin(b, h, i) // dec_steps, 0, 0))

    def page_spec(shape, r):
        def index(b, h, i, pt):
            t = lin(b, h, i)
            return (pt[t // dec_steps, (t % dec_steps) * pages + r], 0, 0)
        return pl.BlockSpec((None,) + shape, index)

    grid_spec = pltpu.PrefetchScalarGridSpec(
        num_scalar_prefetch=1, grid=(n_seq, N_HEADS, n_q),
        in_specs=[pl.BlockSpec((4, HEAD_DIM), lambda b, h, i, pt: (0, 0)),
                  pl.BlockSpec((1, V_DIM), lambda b, h, i, pt: (0, 0)), q_spec,
                  pl.BlockSpec((1, seq // ck, V_DIM, ck), lambda b, h, i, pt: (b, 0, h, 0)),
                  pl.BlockSpec((1, seq, V_DIM), lambda b, h, i, pt: (b, 0, h)),
                  d_spec, d_spec, d_spec]
        + [page_spec((D_ATT, page), r) for r in range(pages)]
        + [page_spec((page * N_HEADS, V_DIM), r) for r in range(pages)],
        out_specs=[q_spec, d_spec],
        scratch_shapes=[pltpu.VMEM((2 * tq, V_DIM), BF16), pltpu.VMEM((2 * tq, LANES), F32),
                        pltpu.VMEM((2 * tq, LANES), F32), pltpu.VMEM((2 * tq, V_DIM), F32),
                        pltpu.VMEM((N_SUB, D_ATT), BF16), pltpu.VMEM((N_SUB, pages * page), F32),
                        pltpu.VMEM((N_SUB, pages * page), BF16), pltpu.VMEM((N_SUB, 1), F32),
                        pltpu.VMEM((N_SUB, 1), F32), pltpu.VMEM((N_SUB, 1), F32),
                        pltpu.VMEM((N_SUB, D_ATT), F32)])
    out, out_d = pl.pallas_call(
        functools.partial(_attn_body, tq=tq, rb_full=rb_full, rb_diag=rb_diag, ck=ck, page=page,
                          pages=pages, dec_steps=dec_steps, lam_init=lam_init),
        out_shape=[jax.ShapeDtypeStruct((n_seq, seq, D_ATT), BF16),
                   jax.ShapeDtypeStruct((n_dec, 1, D_ATT), F32)],
        grid_spec=grid_spec, compiler_params=_params(3), name="attention")(
            page_table, lq, sg.reshape(1, V_DIM), q3, ktb, v3, qd.reshape(n_dec, 1, D_ATT),
            kd.reshape(n_dec, 1, D_ATT), vd.reshape(n_dec, 1, D_ATT),
            *([c_k] * pages), *([c_v] * pages))
    return out.reshape(n_seq * seq, D_ATT), out_d.reshape(n_dec, D_ATT)


def _attn_out_decode_body(x_ref, a_ref, w_ref, sg_ref, out_ref, *, lam_init):
    a = a_ref[...]
    sg = sg_ref[...]
    blocks = []
    for hd in range(N_HEADS):
        blk = a[:, hd * V_DIM:(hd + 1) * V_DIM]
        blocks.append(_rms(blk, sg, SUBLN_EPS) * (1.0 - lam_init))
    a = jnp.concatenate(blocks, axis=1).astype(BF16)
    out_ref[...] = x_ref[...] + jnp.dot(a, w_ref[...], preferred_element_type=F32)


def _attn_out_decode(x, a, w, sg, *, lam_init):
    n = x.shape[0]
    full = lambda shape: pl.BlockSpec(shape, lambda i: (0,) * len(shape))
    return pl.pallas_call(
        functools.partial(_attn_out_decode_body, lam_init=lam_init),
        out_shape=jax.ShapeDtypeStruct((n, D_MODEL), F32), grid=(1,),
        in_specs=[full((n, D_MODEL)), full((n, D_ATT)), full((D_ATT, D_MODEL)), full((1, V_DIM))],
        out_specs=full((n, D_MODEL)),
        compiler_params=_params(1), name="attn_out_decode")(x, a, w, sg.reshape(1, V_DIM))


def kernel(x_prompt, x_sample, cache_conv, cache_k, cache_v, page_table, norm_g, ffn_w_gate,
           ffn_w_up, ffn_w_down, w_in_mix, conv_w, sg_ln_g, sg_ln_b, sg_w, sg_b, w_out_mix, w_qkv,
           lambda_qk, subln_g, w_o, final_norm_g):
    n_seq, seq, _ = x_prompt.shape
    n_dec, dec_seq, _ = x_sample.shape
    tm = 512
    assert dec_seq == 1 and seq % tm == 0
    depth = norm_g.shape[0]
    past_len = page_table.shape[1] * cache_k.shape[2]
    assert past_len % CHUNK == 0

    wg, wu, wd = ffn_w_gate, ffn_w_up, ffn_w_down
    win, wout = w_in_mix.astype(BF16), w_out_mix.astype(BF16)
    wqkv, wo = w_qkv.astype(BF16), w_o.astype(BF16)

    xp = x_prompt.reshape(n_seq * seq, D_MODEL)
    xs = x_sample.reshape(n_dec, D_MODEL)
    ffn = functools.partial(_ffn, tm=tm, ff_chunk=256)
    pos_p = jnp.arange(seq)
    pos_s = jnp.broadcast_to(past_len + jnp.arange(dec_seq), (n_dec,))
    tables_p, tables_s = _rope_tables(pos_p), _rope_tables(pos_s)
    tables_pt = tuple(a.T for a in _rope_angles(pos_p))

    conv_p, conv_s, chunk_s, k_p, v_p, k_s, v_s = [], [], [], [], [], [], []
    attn_p = None
    for l in range(depth):
        last = l == depth - 1
        xp, xs = ffn(xp, xs, norm_g[l, 0], wg, wu, wd, (l, 0))
        if l % 2 == 0:
            i = l // 2
            sgb_tbl = jnp.broadcast_to(sg_b[i][:, :, None], (G_B, CHUNK, LANES))
            xp, st = _mix_prompt(xp, norm_g[l, 1], win[i], conv_w[i], sg_ln_g[i], sg_ln_b[i],
                                 sg_w[i], sgb_tbl, wout[i], n_seq=n_seq, seq=seq, tm=2 * tm)
            conv_p.append(st)
            w00 = jnp.repeat(sg_w[i][:, 0, 0], D_B // G_B).reshape(1, D_B)
            b0 = jnp.repeat(sg_b[i][:, 0], D_B // G_B).reshape(1, D_B)
            xs, st, vn = _mix_decode(xs, norm_g[l, 1], win[i], conv_w[i],
                                     cache_conv[i].reshape(n_dec, (CONV_W - 1) * D_A),
                                     sg_ln_g[i], sg_ln_b[i], w00, b0, wout[i])
            conv_s.append(st.reshape(n_dec, CONV_W - 1, D_A))
            chunk_s.append(vn.reshape(n_dec, dec_seq, D_B))
        else:
            j = l // 2
            lam_init = 0.8 - 0.6 * math.exp(-0.3 * l)
            wq = wqkv[j][:, 0:D_ATT]
            wkt = wqkv[j][:, D_ATT:2 * D_ATT].T
            wv = wqkv[j][:, 2 * D_ATT:]
            q, kt, ktb, v, vb = _qkv_prompt(xp, norm_g[l, 1], wq, wkt, wv, tables_p, tables_pt,
                                            n_seq=n_seq, seq=seq, tm=tm)
            qd, kd, vd = _qkv_decode(xs, norm_g[l, 1], wqkv[j], tables_s)
            a, ad = _attention(q, ktb, vb, qd, kd, vd, cache_k[j], cache_v[j], page_table,
                               lambda_qk[j], subln_g[j], n_seq=n_seq, seq=seq, tq=1024,
                               rb_full=512, rb_diag=256, lam_init=lam_init)
            attn_p = (a, wo[j])
            xs = _attn_out_decode(xs, ad, wo[j], subln_g[j], lam_init=lam_init)
            k_p.append(jnp.transpose(kt.reshape(n_seq, N_SUB, HEAD_DIM, seq), (0, 3, 1, 2)))
            v_p.append(v.reshape(n_seq, seq, N_HEADS, V_DIM))
            k_s.append(kd.reshape(n_dec, dec_seq, N_SUB, HEAD_DIM))
            v_s.append(vd.reshape(n_dec, dec_seq, N_HEADS, V_DIM))
        fg = final_norm_g if last else None
        if attn_p is None:
            xp, xs = ffn(xp, xs, norm_g[l, 2], wg, wu, wd, (l, 1), final_g=fg)
        else:
            xp, xs = ffn(xp, xs, norm_g[l, 2], wg, wu, wd, (l, 1), attn=attn_p[0],
                         w_o=attn_p[1], final_g=fg)
            attn_p = None

    return (xp.reshape(n_seq, seq, D_MODEL), xs.reshape(n_dec, dec_seq, D_MODEL),
            jnp.stack(conv_p), jnp.stack(conv_s), jnp.stack(chunk_s),
            jnp.stack(k_p), jnp.stack(v_p), jnp.stack(k_s), jnp.stack(v_s))
```
